```python
import jax, jax.numpy as jnp
from jax import lax
import numpy as np

D_MODEL = 1024
BATCH = 16
SEQ = 256
DEPTH = 4
DEC_BATCH = 2
DEC_SEQ = 1024
PAST_LEN = 256

GRID_W = 64
NA_HEADS = 8
NA_HEAD_DIM = D_MODEL // 16
NA_WIDTH = NA_HEADS * NA_HEAD_DIM
NA_KH = 8
NA_KW = 16
POOL_WINDOWS = (2, 4, 8, 16)
POOL_GROUPS = 4
POOL_WIDTH = D_MODEL // 4
POOL_GROUP_DIM = POOL_WIDTH // POOL_GROUPS
CONV_WIDTH = D_MODEL // 4
CONV_K = 31
N_BRANCH = 3
N_IN = 3 * NA_WIDTH + POOL_WIDTH + 2 * CONV_WIDTH + N_BRANCH * D_MODEL
FFN_HIDDEN = ((8 * D_MODEL + 3 * 256 - 1) // (3 * 256)) * 256
N_MOD = 6
EPS = 1e-6
NEG_INF = -1e30

kernel_name = "hybrid_na_pool_conv_prefix_dit_step"


def _rmsnorm(x, g):
    xf = x.astype(jnp.float32)
    y = xf * lax.rsqrt(jnp.mean(xf * xf, axis=-1, keepdims=True) + EPS)
    return (y * g.astype(jnp.float32)).astype(x.dtype)


def _layernorm(x, g, b):
    xf = x.astype(jnp.float32)
    mu = jnp.mean(xf, axis=-1, keepdims=True)
    xc = xf - mu
    y = xc * lax.rsqrt(jnp.mean(xc * xc, axis=-1, keepdims=True) + EPS)
    return (y * g.astype(jnp.float32) + b.astype(jnp.float32)).astype(x.dtype)


def _modulation(cond, w_mod, b_mod):
    m = jax.nn.silu(cond) @ w_mod + b_mod
    return jnp.split(m[:, None, :], N_MOD, axis=-1)


def _split_in(u):
    B, L, _ = u.shape
    bounds = np.cumsum([NA_WIDTH, NA_WIDTH, NA_WIDTH, POOL_WIDTH, 2 * CONV_WIDTH]).tolist()
    q, k, v, p_in, c_in, g_in = jnp.split(u, bounds, axis=-1)
    shp = (B, L, NA_HEADS, NA_HEAD_DIM)
    return q.reshape(shp), k.reshape(shp), v.reshape(shp), p_in, c_in, g_in


def _context_attention(q, k, v):
    B, L, H, dh = q.shape
    s = jnp.einsum('bqhd,bkhd->bhqk', q, k).astype(jnp.float32) * (dh ** -0.5)
    p = jax.nn.softmax(s, axis=-1).astype(v.dtype)
    return jnp.einsum('bhqk,bkhd->bqhd', p, v).reshape(B, L, H * dh)


def _neighbourhood_attention(q, k, v, k_ctx, v_ctx, rpb):
    B, L, H, dh = q.shape
    rows = L // GRID_W
    kh = min(NA_KH, rows)
    r = jnp.arange(rows)
    row_start = jnp.clip(r - kh // 2, 0, rows - kh)
    row_idx = row_start[:, None] + jnp.arange(kh)[None, :]
    c = jnp.arange(GRID_W)
    col_start = jnp.clip(c - NA_KW // 2, 0, GRID_W - NA_KW)
    col_in = (c[None, :] >= col_start[:, None]) & (c[None, :] < col_start[:, None] + NA_KW)
    qg = q.reshape(B, rows, GRID_W, H, dh)
    kg = k.reshape(B, rows, GRID_W, H, dh)[:, row_idx]
    vg = v.reshape(B, rows, GRID_W, H, dh)[:, row_idx]
    scale = dh ** -0.5
    s_loc = jnp.einsum('brqhd,brjkhd->bhrqjk', qg, kg).astype(jnp.float32) * scale
    ri = (row_idx - r[:, None]) + NA_KH - 1
    ci = jnp.clip(c[None, :] - c[:, None], -(NA_KW - 1), NA_KW - 1) + NA_KW - 1
    bias = rpb[:, ri[:, None, :, None], ci[None, :, None, :]]
    s_loc = jnp.where(col_in[None, None, None, :, None, :],
                      s_loc + bias[None].astype(jnp.float32), NEG_INF)
    s_ctx = jnp.einsum('brqhd,bchd->bhrqc', qg, k_ctx).astype(jnp.float32) * scale
    n_loc = kh * GRID_W
    s = jnp.concatenate([s_loc.reshape(B, H, rows, GRID_W, n_loc), s_ctx], axis=-1)
    p = jax.nn.softmax(s, axis=-1)
    p_loc = p[..., :n_loc].reshape(B, H, rows, GRID_W, kh, GRID_W).astype(v.dtype)
    p_ctx = p[..., n_loc:].astype(v.dtype)
    o = (jnp.einsum('bhrqjk,brjkhd->brqhd', p_loc, vg)
         + jnp.einsum('bhrqc,bchd->brqhd', p_ctx, v_ctx))
    return o.reshape(B, L, H * dh)


def _pool_mixer(p_in, w_pool, pool_scale):
    B, L, C = p_in.shape
    pf = p_in.astype(jnp.float32)
    cs = jnp.concatenate([jnp.zeros((B, 1, C), jnp.float32), lax.cumsum(pf, axis=1)], axis=1)
    t = jnp.arange(L)
    outs = []
    for g, w in enumerate(POOL_WINDOWS):
        lo = jnp.clip(t - w // 2, 0, L)
        hi = jnp.clip(t - w // 2 + w, 0, L)
        seg = cs[:, :, g * POOL_GROUP_DIM:(g + 1) * POOL_GROUP_DIM]
        cnt = (hi - lo).astype(jnp.float32)[None, :, None]
        outs.append((seg[:, hi] - seg[:, lo]) / cnt)
    pooled = (jnp.concatenate(outs, axis=-1) - pf).astype(p_in.dtype)
    y = jnp.einsum('blgc,gcd->blgd', pooled.reshape(B, L, POOL_GROUPS, POOL_GROUP_DIM), w_pool)
    return y.reshape(B, L, POOL_WIDTH) * pool_scale


def _conv_mixer(c_in, w_dw, b_dw, conv_norm_g, conv_norm_b):
    a, g = jnp.split(c_in, 2, axis=-1)
    h = a * jax.nn.sigmoid(g)
    h = lax.conv_general_dilated(h, w_dw[:, None, :], window_strides=(1,),
                                 padding=[(CONV_K // 2, CONV_K // 2)],
                                 dimension_numbers=('NWC', 'WIO', 'NWC'),
                                 feature_group_count=CONV_WIDTH) + b_dw
    h = _layernorm(h, conv_norm_g, conv_norm_b)
    return jax.nn.silu(h)


def _layer(x, mod, k_ctx, v_ctx, norm1_g, norm2_g, w_in, b_gate, rpb, w_oa, w_pool, pool_scale,
           w_ob, w_dw, b_dw, conv_norm_g, conv_norm_b, w_oc, w_out, w_gu, w_down):
    shift1, scale1, gate1, shift2, scale2, gate2 = mod
    B, L, D = x.shape
    h = _rmsnorm(x, norm1_g) * (1 + scale1) + shift1
    u = h @ w_in
    q, k, v, p_in, c_in, g_in = _split_in(u)
    if k_ctx is None:
        y_a = _context_attention(q, k, v)
        kv_out = (k, v)
    else:
        y_a = _neighbourhood_attention(q, k, v, k_ctx, v_ctx, rpb)
        kv_out = None
    y_a = y_a @ w_oa
    y_b = _pool_mixer(p_in, w_pool, pool_scale) @ w_ob
    y_c = _conv_mixer(c_in, w_dw, b_dw, conv_norm_g, conv_norm_b) @ w_oc
    gates = jax.nn.sigmoid(g_in + b_gate).reshape(B, L, N_BRANCH, D)
    merged = gates[:, :, 0] * y_a + gates[:, :, 1] * y_b + gates[:, :, 2] * y_c
    x = x + gate1 * (merged @ w_out)
    h2 = _rmsnorm(x, norm2_g) * (1 + scale2) + shift2
    a, b = jnp.split(h2 @ w_gu, 2, axis=-1)
    x = x + gate2 * ((jax.nn.silu(a) * b) @ w_down)
    return x, kv_out


def setup_inputs(seed: int = 0) -> dict:
    key = jax.random.key(seed)
    ks = jax.random.split(key, 32)

    def nrm(k, shape, scale):
        return scale * jax.random.normal(k, shape, jnp.float32)

    D = D_MODEL
    kv_shape = (DEC_BATCH, DEPTH, PAST_LEN, NA_HEADS, NA_HEAD_DIM)
    return {
        "x_prompt": nrm(ks[0], (BATCH, SEQ, D), 1.0),
        "x_sample": nrm(ks[1], (DEC_BATCH, DEC_SEQ, D), 1.0),
        "cache_k": nrm(ks[2], kv_shape, 1.0),
        "cache_v": nrm(ks[3], kv_shape, 1.0),
        "c": nrm(ks[4], (DEC_BATCH, D), 1.0),
        "c_ctx": nrm(ks[5], (D,), 1.0),
        "w_mod": nrm(ks[6], (DEPTH, D, N_MOD * D), 0.5 * D ** -0.5),
        "b_mod": nrm(ks[7], (DEPTH, N_MOD * D), 0.02),
        "norm1_g": 1.0 + nrm(ks[8], (DEPTH, D), 0.05),
        "norm2_g": 1.0 + nrm(ks[9], (DEPTH, D), 0.05),
        "w_in": nrm(ks[10], (DEPTH, D, N_IN), D ** -0.5),
        "b_gate": nrm(ks[11], (DEPTH, N_BRANCH * D), 0.02),
        "rpb": nrm(ks[12], (DEPTH, NA_HEADS, 2 * NA_KH - 1, 2 * NA_KW - 1), 0.1),
        "w_oa": nrm(ks[13], (DEPTH, NA_WIDTH, D), NA_WIDTH ** -0.5),
        "w_pool": nrm(ks[14], (DEPTH, POOL_GROUPS, POOL_GROUP_DIM, POOL_GROUP_DIM), POOL_GROUP_DIM ** -0.5),
        "pool_scale": 1.0 + nrm(ks[15], (DEPTH, POOL_WIDTH), 0.05),
        "w_ob": nrm(ks[16], (DEPTH, POOL_WIDTH, D), POOL_WIDTH ** -0.5),
        "w_dw": nrm(ks[17], (DEPTH, CONV_K, CONV_WIDTH), CONV_K ** -0.5),
        "b_dw": nrm(ks[18], (DEPTH, CONV_WIDTH), 0.02),
        "conv_norm_g": 1.0 + nrm(ks[19], (DEPTH, CONV_WIDTH), 0.05),
        "conv_norm_b": nrm(ks[20], (DEPTH, CONV_WIDTH), 0.02),
        "w_oc": nrm(ks[21], (DEPTH, CONV_WIDTH, D), CONV_WIDTH ** -0.5),
        "w_out": nrm(ks[22], (DEPTH, D, D), D ** -0.5),
        "w_gu": nrm(ks[23], (DEPTH, D, 2 * FFN_HIDDEN), D ** -0.5),
        "w_down": nrm(ks[24], (DEPTH, FFN_HIDDEN, D), FFN_HIDDEN ** -0.5),
        "final_g": 1.0 + nrm(ks[25], (D,), 0.05),
    }


def reference(x_prompt, x_sample, cache_k, cache_v, c, c_ctx, w_mod, b_mod, norm1_g, norm2_g,
              w_in, b_gate, rpb, w_oa, w_pool, pool_scale, w_ob, w_dw, b_dw, conv_norm_g,
              conv_norm_b, w_oc, w_out, w_gu, w_down, final_g):
    xp = x_prompt
    xs = x_sample
    ks_out, vs_out = [], []
    for l in range(DEPTH):
        prm = dict(norm1_g=norm1_g[l], norm2_g=norm2_g[l], w_in=w_in[l], b_gate=b_gate[l],
                   rpb=rpb[l], w_oa=w_oa[l], w_pool=w_pool[l], pool_scale=pool_scale[l],
                   w_ob=w_ob[l], w_dw=w_dw[l], b_dw=b_dw[l], conv_norm_g=conv_norm_g[l],
                   conv_norm_b=conv_norm_b[l], w_oc=w_oc[l], w_out=w_out[l], w_gu=w_gu[l],
                   w_down=w_down[l])
        mod_ctx = _modulation(c_ctx[None, :], w_mod[l], b_mod[l])
        mod_lat = _modulation(c, w_mod[l], b_mod[l])
        xp, kv = _layer(xp, mod_ctx, None, None, **prm)
        xs, _ = _layer(xs, mod_lat, cache_k[:, l], cache_v[:, l], **prm)
        ks_out.append(kv[0])
        vs_out.append(kv[1])
    y_prompt = _rmsnorm(xp, final_g)
    y_sample = _rmsnorm(xs, final_g)
    new_k = jnp.stack(ks_out, axis=1)
    new_v = jnp.stack(vs_out, axis=1)
    return (y_prompt, y_sample, new_k, new_v)
```

```python
import functools

import jax
import jax.numpy as jnp
from jax import lax
from jax.experimental import pallas as pl
from jax.experimental.pallas import tpu as pltpu

D_MODEL = 1024
BATCH = 16
SEQ = 256
DEPTH = 4
DEC_BATCH = 2
DEC_SEQ = 1024
PAST_LEN = 256
GRID_W = 64
GRID_ROWS = DEC_SEQ // GRID_W
NA_HEADS = 8
NA_HEAD_DIM = 64
NA_WIDTH = NA_HEADS * NA_HEAD_DIM
NA_KH = 8
NA_KW = 16
POOL_WIDTH = 256
POOL_GROUPS = 4
POOL_GROUP_DIM = 64
CONV_WIDTH = 256
CONV_K = 31
N_BRANCH = 3
N_MIX = 3 * NA_WIDTH + POOL_WIDTH + 2 * CONV_WIDTH
N_GATE = N_BRANCH * D_MODEL
FFN_HIDDEN = 2816
N_MOD = 6
EPS = 1e-6
NEG_INF = -1e30

N_PROMPT_TOK = BATCH * SEQ
N_SAMPLE_TOK = DEC_BATCH * DEC_SEQ
N_TOK = N_PROMPT_TOK + N_SAMPLE_TOK

LANES = 128
MOD_ROWS = 8
MOD_BLOCK_N = 1536
TOK_TILE = 256
ATT_BLOCK_ROWS = 4
ATT_KEY_ROWS = 12
BIAS_TILES = 32
POOL_PAD = 8
CONV_PAD = 16
MIX_CHUNK = 128
FFN_CHUNKS = ((0, 1024), (1024, 2048), (2048, 2816))
VMEM_LIMIT = 56 * 1024 * 1024

F32 = jnp.float32
BF16 = jnp.bfloat16


def _sigmoid(x):
    return 1.0 / (1.0 + jnp.exp(-x))


def _rms(x):
    return x * lax.rsqrt(jnp.mean(x * x, axis=-1, keepdims=True) + EPS)


def _dot(a, b):
    return jnp.dot(a, b, preferred_element_type=F32)


def _dot_nt(a, b):
    return lax.dot_general(a, b, (((1,), (1,)), ((), ())), preferred_element_type=F32)


def _const_spec(shape):
    zeros = (0,) * len(shape)
    return pl.BlockSpec(shape, lambda *_: zeros, pipeline_mode=pl.Buffered(1))


def _mod_kernel(cond_ref, w_ref, b_ref, out_ref):
    c = cond_ref[...]
    s = (c * _sigmoid(c)).astype(BF16)
    out_ref[...] = _dot(s, w_ref[...].astype(BF16)) + b_ref[...]


def _modulation(cond, w_mod, b_mod):
    n = N_MOD * D_MODEL
    return pl.pallas_call(
        _mod_kernel,
        grid=(DEPTH, n // MOD_BLOCK_N),
        in_specs=[
            pl.BlockSpec((MOD_ROWS, D_MODEL), lambda l, j: (0, 0)),
            pl.BlockSpec((None, D_MODEL, MOD_BLOCK_N), lambda l, j: (l, 0, j)),
            pl.BlockSpec((None, 1, MOD_BLOCK_N), lambda l, j: (l, 0, j)),
        ],
        out_specs=pl.BlockSpec((None, MOD_ROWS, MOD_BLOCK_N), lambda l, j: (l, 0, j)),
        out_shape=jax.ShapeDtypeStruct((DEPTH, MOD_ROWS, n), F32),
        compiler_params=pltpu.CompilerParams(
            dimension_semantics=("arbitrary", "arbitrary"), vmem_limit_bytes=VMEM_LIMIT),
        name="modulation",
    )(cond, w_mod, b_mod.reshape(DEPTH, 1, n))


def _bias_kernel(rpb_ref, out_ref):
    rp = rpb_ref[...]
    lane = lax.broadcasted_iota(jnp.int32, rp.shape, 1)
    lo = GRID_W - NA_KW
    rolled = pltpu.roll(rp, lo, axis=1)
    first = rp[:, 0:1]
    last = rp[:, 2 * NA_KW - 2:2 * NA_KW - 1]
    ext = jnp.where(lane < lo, first, jnp.where(lane > lo + 2 * NA_KW - 2, last, rolled))

    cq = lax.broadcasted_iota(jnp.int32, (GRID_W, LANES), 0)
    ln = lax.broadcasted_iota(jnp.int32, (GRID_W, LANES), 1)
    ck = jnp.where(ln < GRID_W, ln, ln - GRID_W)
    cs = jnp.clip(cq - NA_KW // 2, 0, GRID_W - NA_KW)
    col_in = (ck >= cs) & (ck < cs + NA_KW)
    low_half = ln < GRID_W
    neg = jnp.full((GRID_W, LANES), NEG_INF, F32)

    def toeplitz(tile, shift):
        dr = tile - 15
        if dr < -(NA_KH - 1) or dr > NA_KH - 1:
            return neg
        row = jnp.broadcast_to(ext[dr + NA_KH - 1:dr + NA_KH, :], (GRID_W, LANES))
        return pltpu.roll(row, shift, axis=1, stride=1, stride_axis=0)

    for copy in range(2):
        for col in range(BIAS_TILES // 2):
            t_lo = 2 * col + copy
            t_hi = t_lo + 1
            a = toeplitz(t_lo, GRID_W + 1)
            b = toeplitz(t_hi, 1)
            pair = jnp.where(low_half, a, b)
            pair = jnp.where(col_in, pair, NEG_INF)
            out_ref[copy, :, col * LANES:(col + 1) * LANES] = pair


def _bias_tables(rpb):
    rp = jnp.pad(rpb, ((0, 0), (0, 0), (0, 1), (0, LANES - (2 * NA_KW - 1))))
    width = BIAS_TILES * GRID_W
    return pl.pallas_call(
        _bias_kernel,
        grid=(DEPTH, NA_HEADS),
        in_specs=[pl.BlockSpec((None, None, 2 * NA_KH, LANES), lambda l, h: (l, h, 0, 0))],
        out_specs=pl.BlockSpec((None, None, 2, GRID_W, width), lambda l, h: (l, h, 0, 0, 0)),
        out_shape=jax.ShapeDtypeStruct((DEPTH, NA_HEADS, 2, GRID_W, width), F32),
        compiler_params=pltpu.CompilerParams(dimension_semantics=("arbitrary", "arbitrary")),
        name="bias_tables",
    )(rp)


def _in_proj(x_ref, mod_ref, row, g1_ref, win_ref, u_ref, n_tok):
    shift1 = mod_ref[pl.ds(row, 1), 0:D_MODEL]
    scale1 = mod_ref[pl.ds(row, 1), D_MODEL:2 * D_MODEL]
    for t0 in range(0, n_tok, TOK_TILE):
        x = x_ref[t0:t0 + TOK_TILE, :]
        h = ((_rms(x) * g1_ref[...]) * (1.0 + scale1) + shift1).astype(BF16)
        u_ref[t0:t0 + TOK_TILE, :] = _dot(h, win_ref[...])


def _softmax_pv(parts):
    m = None
    for s, _ in parts:
        mi = jnp.max(s, axis=-1, keepdims=True)
        m = mi if m is None else jnp.maximum(m, mi)
    den = None
    acc = None
    for s, v in parts:
        e = jnp.exp(s - m)
        li = jnp.sum(e, axis=-1, keepdims=True)
        oi = _dot(e.astype(BF16), v)
        den = li if den is None else den + li
        acc = oi if acc is None else acc + oi
    return acc / den


def _context_attention(u_ref, kv_ref, y_ref, n_tok):
    scale = NA_HEAD_DIM ** -0.5
    lane = lax.broadcasted_iota(jnp.int32, (n_tok, LANES), 1)
    low = lane < NA_HEAD_DIM
    for hp in range(NA_HEADS // 2):
        c0 = hp * LANES
        qpair = u_ref[:, c0:c0 + LANES] * scale
        kpair = kv_ref[:, c0:c0 + LANES]
        vpair = kv_ref[:, NA_WIDTH + c0:NA_WIDTH + c0 + LANES]
        outs = []
        for hh in range(2):
            q = jnp.where(low if hh == 0 else ~low, qpair, 0.0).astype(BF16)
            outs.append(_softmax_pv([(_dot_nt(q, kpair), vpair)]))
        y_ref[:, c0:c0 + LANES] = jnp.where(low, outs[0], outs[1]).astype(BF16)


def _window_start(r):
    return min(max(r - NA_KH // 2, 0), GRID_ROWS - NA_KH)


def _neighbourhood_attention(u_ref, kv_ref, ctx_ref, bias_ref, y_ref):
    scale = NA_HEAD_DIM ** -0.5
    blk_tok = ATT_BLOCK_ROWS * GRID_W
    key_tok = ATT_KEY_ROWS * GRID_W
    lane = lax.broadcasted_iota(jnp.int32, (blk_tok, LANES), 1)
    low = lane < NA_HEAD_DIM
    klane = lax.broadcasted_iota(jnp.int32, (GRID_W, key_tok), 1)
    for blk in range(GRID_ROWS // ATT_BLOCK_ROWS):
        r0 = blk * ATT_BLOCK_ROWS
        j0 = min(_window_start(r0), GRID_ROWS - ATT_KEY_ROWS)
        assert _window_start(r0 + ATT_BLOCK_ROWS - 1) + NA_KH <= j0 + ATT_KEY_ROWS
        q0 = r0 * GRID_W
        k0 = j0 * GRID_W
        for hp in range(NA_HEADS // 2):
            c0 = hp * LANES
            qpair = u_ref[q0:q0 + blk_tok, c0:c0 + LANES] * scale
            kpair = kv_ref[k0:k0 + key_tok, c0:c0 + LANES]
            vpair = kv_ref[k0:k0 + key_tok, NA_WIDTH + c0:NA_WIDTH + c0 + LANES]
            kctx = ctx_ref[:, c0:c0 + LANES]
            vctx = ctx_ref[:, NA_WIDTH + c0:NA_WIDTH + c0 + LANES]
            outs = []
            for hh in range(2):
                head = 2 * hp + hh
                q = jnp.where(low if hh == 0 else ~low, qpair, 0.0).astype(BF16)
                s_loc = _dot_nt(q, kpair)
                s_ctx = _dot_nt(q, kctx)
                strips = []
                for rr in range(ATT_BLOCK_ROWS):
                    r = r0 + rr
                    rs = _window_start(r)
                    tile0 = 15 + j0 - r
                    copy = tile0 % 2
                    b0 = (tile0 - copy) * GRID_W
                    bias = bias_ref[head, copy, :, b0:b0 + key_tok]
                    valid = (klane >= (rs - j0) * GRID_W) & (klane < (rs - j0 + NA_KH) * GRID_W)
                    s = s_loc[rr * GRID_W:(rr + 1) * GRID_W, :] + bias
                    strips.append(jnp.where(valid, s, NEG_INF))
                s_loc = jnp.concatenate(strips, axis=0)
                outs.append(_softmax_pv([(s_loc, vpair), (s_ctx, vctx)]))
            y_ref[q0:q0 + blk_tok, c0:c0 + LANES] = jnp.where(low, outs[0], outs[1]).astype(BF16)


def _pool_mixer(u_ref, pp_ref, wpool_ref, pscale_ref, y_ref, n_tok):
    c_in = 3 * NA_WIDTH
    zeros = jnp.zeros((POOL_PAD, POOL_WIDTH), F32)
    pp_ref[0:POOL_PAD, :] = zeros
    pp_ref[POOL_PAD + n_tok:2 * POOL_PAD + n_tok, :] = zeros
    pp_ref[POOL_PAD:POOL_PAD + n_tok, :] = u_ref[:, c_in:c_in + POOL_WIDTH]
    lane = lax.broadcasted_iota(jnp.int32, (MIX_CHUNK, LANES), 1)
    first = lane < POOL_GROUP_DIM
    tok = lax.broadcasted_iota(jnp.int32, (MIX_CHUNK, LANES), 0)

    for t0 in range(0, n_tok, MIX_CHUNK):
        def ld(d, half):
            a = POOL_PAD + t0 + d
            return pp_ref[a:a + MIX_CHUNK, half * LANES:(half + 1) * LANES]

        def window(half, lo, hi):
            acc = None
            for d in range(lo, hi):
                acc = ld(d, half) if acc is None else acc + ld(d, half)
            return acc

        a2 = window(0, -1, 1)
        a4 = a2 + ld(-2, 0) + ld(1, 0)
        a8 = window(1, -4, 4)
        a16 = a8 + window(1, -8, -4) + window(1, 4, 8)
        t = tok + t0
        halves = []
        for half, (small, big, ws, wb) in enumerate(((a2, a4, 1, 2), (a8, a16, 4, 8))):
            hw = jnp.where(first, ws, wb)
            cnt = (jnp.minimum(t + hw, n_tok) - jnp.maximum(t - hw, 0)).astype(F32)
            halves.append(jnp.where(first, small, big) / cnt - ld(0, half))
        pooled = jnp.concatenate(halves, axis=1).astype(BF16)
        y = _dot(pooled, wpool_ref[...]) * pscale_ref[...]
        y_ref[t0:t0 + MIX_CHUNK, NA_WIDTH:NA_WIDTH + POOL_WIDTH] = y.astype(BF16)


def _conv_mixer(u_ref, hp_ref, wdw_ref, bdw_ref, cng_ref, cnb_ref, y_ref, n_tok):
    c_a = 3 * NA_WIDTH + POOL_WIDTH
    c_g = c_a + CONV_WIDTH
    zeros = jnp.zeros((CONV_PAD, CONV_WIDTH), F32)
    hp_ref[0:CONV_PAD, :] = zeros
    hp_ref[CONV_PAD + n_tok:2 * CONV_PAD + n_tok, :] = zeros
    for t0 in range(0, n_tok, TOK_TILE):
        a = u_ref[t0:t0 + TOK_TILE, c_a:c_a + CONV_WIDTH]
        g = u_ref[t0:t0 + TOK_TILE, c_g:c_g + CONV_WIDTH]
        hp_ref[CONV_PAD + t0:CONV_PAD + t0 + TOK_TILE, :] = a * _sigmoid(g)
    off = CONV_PAD - CONV_K // 2
    for t0 in range(0, n_tok, MIX_CHUNK):
        acc = None
        for k in range(CONV_K):
            term = wdw_ref[k:k + 1, :] * hp_ref[t0 + off + k:t0 + off + k + MIX_CHUNK, :]
            acc = term if acc is None else acc + term
        acc = acc + bdw_ref[...]
        xc = acc - jnp.mean(acc, axis=-1, keepdims=True)
        y = xc * lax.rsqrt(jnp.mean(xc * xc, axis=-1, keepdims=True) + EPS)
        y = y * cng_ref[...] + cnb_ref[...]
        y_ref[t0:t0 + MIX_CHUNK, NA_WIDTH + POOL_WIDTH:D_MODEL] = (y * _sigmoid(y)).astype(BF16)


def _mix_prompt_kernel(x_ref, mod_ref, g1_ref, win_ref, wpool_ref, pscale_ref, wdw_ref, bdw_ref,
                       cng_ref, cnb_ref, y_ref, k_ref, v_ref, u_ref, kv_ref, pp_ref, hp_ref):
    _in_proj(x_ref, mod_ref, 0, g1_ref, win_ref, u_ref, SEQ)
    k_ref[...] = u_ref[:, NA_WIDTH:2 * NA_WIDTH]
    v_ref[...] = u_ref[:, 2 * NA_WIDTH:3 * NA_WIDTH]
    kv_ref[...] = u_ref[:, NA_WIDTH:3 * NA_WIDTH].astype(BF16)
    _context_attention(u_ref, kv_ref, y_ref, SEQ)
    _pool_mixer(u_ref, pp_ref, wpool_ref, pscale_ref, y_ref, SEQ)
    _conv_mixer(u_ref, hp_ref, wdw_ref, bdw_ref, cng_ref, cnb_ref, y_ref, SEQ)


def _mix_sample_kernel(x_ref, mod_ref, g1_ref, win_ref, wpool_ref, pscale_ref, wdw_ref, bdw_ref,
                       cng_ref, cnb_ref, ck_ref, cv_ref, bias_ref, y_ref,
                       u_ref, kv_ref, ctx_ref, pp_ref, hp_ref):
    row = 1 + pl.program_id(0)
    _in_proj(x_ref, mod_ref, row, g1_ref, win_ref, u_ref, DEC_SEQ)
    kv_ref[...] = u_ref[:, NA_WIDTH:3 * NA_WIDTH].astype(BF16)
    ctx_ref[:, 0:NA_WIDTH] = ck_ref[...].astype(BF16)
    ctx_ref[:, NA_WIDTH:2 * NA_WIDTH] = cv_ref[...].astype(BF16)
    _neighbourhood_attention(u_ref, kv_ref, ctx_ref, bias_ref, y_ref)
    _pool_mixer(u_ref, pp_ref, wpool_ref, pscale_ref, y_ref, DEC_SEQ)
    _conv_mixer(u_ref, hp_ref, wdw_ref, bdw_ref, cng_ref, cnb_ref, y_ref, DEC_SEQ)


def _mixer_weight_specs():
    return [
        _const_spec((MOD_ROWS, N_MOD * D_MODEL)),
        _const_spec((1, D_MODEL)),
        _const_spec((D_MODEL, N_MIX)),
        _const_spec((POOL_WIDTH, POOL_WIDTH)),
        _const_spec((1, POOL_WIDTH)),
        _const_spec((CONV_K, CONV_WIDTH)),
        _const_spec((1, CONV_WIDTH)),
        _const_spec((1, CONV_WIDTH)),
        _const_spec((1, CONV_WIDTH)),
    ]


def _mix_scratch(n_tok):
    return [
        pltpu.VMEM((n_tok, N_MIX), F32),
        pltpu.VMEM((n_tok, 2 * NA_WIDTH), BF16),
    ], [
        pltpu.VMEM((n_tok + 2 * POOL_PAD, POOL_WIDTH), F32),
        pltpu.VMEM((n_tok + 2 * CONV_PAD, CONV_WIDTH), F32),
    ]


def _mix_prompt(x_all, mod, weights):
    head, tail = _mix_scratch(SEQ)
    return pl.pallas_call(
        _mix_prompt_kernel,
        grid=(BATCH,),
        in_specs=[pl.BlockSpec((SEQ, D_MODEL), lambda b: (b, 0))] + _mixer_weight_specs(),
        out_specs=[
            pl.BlockSpec((SEQ, D_MODEL), lambda b: (b, 0)),
            pl.BlockSpec((None, SEQ, NA_WIDTH), lambda b: (b, 0, 0)),
            pl.BlockSpec((None, SEQ, NA_WIDTH), lambda b: (b, 0, 0)),
        ],
        out_shape=[
            jax.ShapeDtypeStruct((N_PROMPT_TOK, D_MODEL), BF16),
            jax.ShapeDtypeStruct((BATCH, SEQ, NA_WIDTH), F32),
            jax.ShapeDtypeStruct((BATCH, SEQ, NA_WIDTH), F32),
        ],
        scratch_shapes=head + tail,
        compiler_params=pltpu.CompilerParams(
            dimension_semantics=("arbitrary",), vmem_limit_bytes=VMEM_LIMIT),
        name="mix_prompt",
    )(x_all, mod, *weights)


def _mix_sample(x_all, mod, weights, cache_k, cache_v, bias):
    head, tail = _mix_scratch(DEC_SEQ)
    first_blk = N_PROMPT_TOK // DEC_SEQ
    ctx_spec = pl.BlockSpec((None, PAST_LEN, NA_WIDTH), lambda b: (b, 0, 0))
    return pl.pallas_call(
        _mix_sample_kernel,
        grid=(DEC_BATCH,),
        in_specs=[pl.BlockSpec((DEC_SEQ, D_MODEL), lambda b: (first_blk + b, 0))]
        + _mixer_weight_specs()
        + [ctx_spec, ctx_spec,
           _const_spec((NA_HEADS, 2, GRID_W, BIAS_TILES * GRID_W))],
        out_specs=pl.BlockSpec((DEC_SEQ, D_MODEL), lambda b: (b, 0)),
        out_shape=jax.ShapeDtypeStruct((N_SAMPLE_TOK, D_MODEL), BF16),
        scratch_shapes=head + [pltpu.VMEM((PAST_LEN, 2 * NA_WIDTH), BF16)] + tail,
        compiler_params=pltpu.CompilerParams(
            dimension_semantics=("arbitrary",), vmem_limit_bytes=VMEM_LIMIT),
        name="mix_sample",
    )(x_all, mod, *weights, cache_k, cache_v, bias)


def _ffn_kernel(final, x_ref, y_ref, mod_ref, g1_ref, g2_ref, fg_ref, wg_ref, bg_ref, woa_ref,
                wob_ref, woc_ref, wout_ref, wgu_ref, wdown_ref, o_ref):
    i = pl.program_id(0)
    prompt_tiles = N_PROMPT_TOK // TOK_TILE
    row = jnp.where(i < prompt_tiles, 0, 1 + (i - prompt_tiles) // (DEC_SEQ // TOK_TILE))

    def mod(j):
        return mod_ref[pl.ds(row, 1), j * D_MODEL:(j + 1) * D_MODEL]

    x = x_ref[...]
    h = ((_rms(x) * g1_ref[...]) * (1.0 + mod(1)) + mod(0)).astype(BF16)
    branches = ((0, NA_WIDTH, woa_ref), (NA_WIDTH, NA_WIDTH + POOL_WIDTH, wob_ref),
                (NA_WIDTH + POOL_WIDTH, D_MODEL, woc_ref))
    merged = None
    for br, (c0, c1, w_ref) in enumerate(branches):
        g0 = br * D_MODEL
        gate = _sigmoid(_dot(h, wg_ref[:, g0:g0 + D_MODEL]) + bg_ref[:, g0:g0 + D_MODEL])
        term = gate * _dot(y_ref[:, c0:c1], w_ref[...])
        merged = term if merged is None else merged + term
    x1 = x + mod(2) * _dot(merged.astype(BF16), wout_ref[...])
    h2 = ((_rms(x1) * g2_ref[...]) * (1.0 + mod(4)) + mod(3)).astype(BF16)
    acc = None
    for c0, c1 in FFN_CHUNKS:
        a = _dot(h2, wgu_ref[:, c0:c1])
        b = _dot(h2, wgu_ref[:, FFN_HIDDEN + c0:FFN_HIDDEN + c1])
        t = ((a * _sigmoid(a)) * b).astype(BF16)
        d = _dot(t, wdown_ref[c0:c1, :])
        acc = d if acc is None else acc + d
    x2 = x1 + mod(5) * acc
    if final:
        x2 = _rms(x2) * fg_ref[...]
    o_ref[...] = x2


def _ffn(x_all, y_all, mod, weights, final):
    tile = pl.BlockSpec((TOK_TILE, D_MODEL), lambda i: (i, 0))
    return pl.pallas_call(
        functools.partial(_ffn_kernel, final),
        grid=(N_TOK // TOK_TILE,),
        in_specs=[tile, tile,
                  _const_spec((MOD_ROWS, N_MOD * D_MODEL)),
                  _const_spec((1, D_MODEL)), _const_spec((1, D_MODEL)), _const_spec((1, D_MODEL)),
                  _const_spec((D_MODEL, N_GATE)), _const_spec((1, N_GATE)),
                  _const_spec((NA_WIDTH, D_MODEL)), _const_spec((POOL_WIDTH, D_MODEL)),
                  _const_spec((CONV_WIDTH, D_MODEL)), _const_spec((D_MODEL, D_MODEL)),
                  _const_spec((D_MODEL, 2 * FFN_HIDDEN)), _const_spec((FFN_HIDDEN, D_MODEL))],
        out_specs=tile,
        out_shape=jax.ShapeDtypeStruct((N_TOK, D_MODEL), F32),
        compiler_params=pltpu.CompilerParams(
            dimension_semantics=("arbitrary",), vmem_limit_bytes=VMEM_LIMIT),
        name="merge_ffn",
    )(x_all, y_all, mod, *weights)


def _block_diag(w_pool):
    out = jnp.zeros((POOL_WIDTH, POOL_WIDTH), w_pool.dtype)
    for g in range(POOL_GROUPS):
        a = g * POOL_GROUP_DIM
        out = out.at[a:a + POOL_GROUP_DIM, a:a + POOL_GROUP_DIM].set(w_pool[g])
    return out


def kernel(x_prompt, x_sample, cache_k, cache_v, c, c_ctx, w_mod, b_mod, norm1_g, norm2_g, w_in,
           b_gate, rpb, w_oa, w_pool, pool_scale, w_ob, w_dw, b_dw, conv_norm_g, conv_norm_b,
           w_oc, w_out, w_gu, w_down, final_g):
    x_all = jnp.concatenate([x_prompt.reshape(N_PROMPT_TOK, D_MODEL),
                             x_sample.reshape(N_SAMPLE_TOK, D_MODEL)], axis=0)
    cond = jnp.concatenate([c_ctx[None, :], c,
                            jnp.zeros((MOD_ROWS - 1 - DEC_BATCH, D_MODEL), F32)], axis=0)
    mods = _modulation(cond, w_mod, b_mod)
    bias = _bias_tables(rpb)
    ck = cache_k.reshape(DEC_BATCH, DEPTH, PAST_LEN, NA_WIDTH)
    cv = cache_v.reshape(DEC_BATCH, DEPTH, PAST_LEN, NA_WIDTH)
    fg = final_g.reshape(1, D_MODEL)
    ks, vs = [], []
    for l in range(DEPTH):
        mix_w = (norm1_g[l].reshape(1, D_MODEL),
                 w_in[l, :, :N_MIX].astype(BF16),
                 _block_diag(w_pool[l]).astype(BF16),
                 pool_scale[l].reshape(1, POOL_WIDTH),
                 w_dw[l],
                 b_dw[l].reshape(1, CONV_WIDTH),
                 conv_norm_g[l].reshape(1, CONV_WIDTH),
                 conv_norm_b[l].reshape(1, CONV_WIDTH))
        y_p, k_l, v_l = _mix_prompt(x_all, mods[l], mix_w)
        y_s = _mix_sample(x_all, mods[l], mix_w, ck[:, l], cv[:, l], bias[l])
        ks.append(k_l)
        vs.append(v_l)
        ffn_w = (norm1_g[l].reshape(1, D_MODEL), norm2_g[l].reshape(1, D_MODEL), fg,
                 w_in[l, :, N_MIX:].astype(BF16), b_gate[l].reshape(1, N_GATE),
                 w_oa[l].astype(BF16), w_ob[l].astype(BF16), w_oc[l].astype(BF16),
                 w_out[l].astype(BF16), w_gu[l].astype(BF16), w_down[l].astype(BF16))
        x_all = _ffn(x_all, jnp.concatenate([y_p, y_s], axis=0), mods[l], ffn_w, l == DEPTH - 1)
    y_prompt = x_all[:N_PROMPT_TOK].reshape(BATCH, SEQ, D_MODEL)
    y_sample = x_all[N_PROMPT_TOK:].reshape(DEC_BATCH, DEC_SEQ, D_MODEL)
    kv_shape = (BATCH, DEPTH, SEQ, NA_HEADS, NA_HEAD_DIM)
    new_k = jnp.stack(ks, axis=1).reshape(kv_shape)
    new_v = jnp.stack(vs, axis=1).reshape(kv_shape)
    return (y_prompt, y_sample, new_k, new_v)
```

```python
import functools

import jax
import jax.numpy as jnp
from jax import lax
from jax.experimental import pallas as pl
from jax.experimental.pallas import tpu as pltpu

D_MODEL = 1024
BATCH = 16
SEQ = 256
DEPTH = 4
DEC_BATCH = 2
DEC_SEQ = 1024
PAST_LEN = 256
GRID_W = 64
GRID_ROWS = DEC_SEQ // GRID_W
NA_HEADS = 8
NA_HEAD_DIM = 64
NA_WIDTH = NA_HEADS * NA_HEAD_DIM
NA_KH = 8
NA_KW = 16
POOL_WIDTH = 256
POOL_GROUPS = 4
POOL_GROUP_DIM = 64
CONV_WIDTH = 256
CONV_K = 31
N_BRANCH = 3
N_MIX = 3 * NA_WIDTH + POOL_WIDTH + 2 * CONV_WIDTH
N_GATE = N_BRANCH * D_MODEL
N_IN = N_MIX + N_GATE
FFN_HIDDEN = 2816
N_MOD = 6
EPS = 1e-6
NEG_INF = -1e30

N_PROMPT_TOK = BATCH * SEQ
N_SAMPLE_TOK = DEC_BATCH * DEC_SEQ
N_TOK = N_PROMPT_TOK + N_SAMPLE_TOK

LANES = 128
MOD_ROWS = 8
MOD_BLOCK_N = 1536
TOK_TILE = 256
ATT_BLOCK_ROWS = 4
ATT_KEY_ROWS = 12
BIAS_TILES = 32
POOL_PAD = 8
CONV_PAD = 16
MIX_CHUNK = 128
FFN_CHUNKS = ((0, 1024), (1024, 2048), (2048, 2816))
VMEM_LIMIT = 56 * 1024 * 1024

F32 = jnp.float32
BF16 = jnp.bfloat16

CAST_WEIGHTS = ((D_MODEL, N_IN), (NA_WIDTH, D_MODEL), (POOL_WIDTH, D_MODEL), (CONV_WIDTH, D_MODEL),
                (D_MODEL, D_MODEL), (D_MODEL, 2 * FFN_HIDDEN), (FFN_HIDDEN, D_MODEL))


def _sigmoid(x):
    return 1.0 / (1.0 + jnp.exp(-x))


def _rms(x):
    return x * lax.rsqrt(jnp.mean(x * x, axis=-1, keepdims=True) + EPS)


def _dot(a, b):
    return jnp.dot(a, b, preferred_element_type=F32)


def _dot_nt(a, b):
    return lax.dot_general(a, b, (((1,), (1,)), ((), ())), preferred_element_type=F32)


def _layer_spec(shape):
    zeros = (0,) * len(shape)
    return pl.BlockSpec((None,) + tuple(shape), lambda i, l: (l[0],) + zeros,
                        pipeline_mode=pl.Buffered(1))


def _whole_spec(shape):
    zeros = (0,) * len(shape)
    return pl.BlockSpec(tuple(shape), lambda i, l: zeros, pipeline_mode=pl.Buffered(1))


def _mod_kernel(cond_ref, w_ref, b_ref, out_ref):
    c = cond_ref[...]
    s = (c * _sigmoid(c)).astype(BF16)
    out_ref[...] = _dot(s, w_ref[...].astype(BF16)) + b_ref[...]


def _modulation(cond, w_mod, b_mod):
    n = N_MOD * D_MODEL
    return pl.pallas_call(
        _mod_kernel,
        grid=(DEPTH, n // MOD_BLOCK_N),
        in_specs=[
            pl.BlockSpec((MOD_ROWS, D_MODEL), lambda l, j: (0, 0)),
            pl.BlockSpec((None, D_MODEL, MOD_BLOCK_N), lambda l, j: (l, 0, j)),
            pl.BlockSpec((None, 1, MOD_BLOCK_N), lambda l, j: (l, 0, j)),
        ],
        out_specs=pl.BlockSpec((None, MOD_ROWS, MOD_BLOCK_N), lambda l, j: (l, 0, j)),
        out_shape=jax.ShapeDtypeStruct((DEPTH, MOD_ROWS, n), F32),
        compiler_params=pltpu.CompilerParams(
            dimension_semantics=("arbitrary", "arbitrary"), vmem_limit_bytes=VMEM_LIMIT),
        name="modulation",
    )(cond, w_mod, b_mod.reshape(DEPTH, 1, n))


def _bias_kernel(rpb_ref, out_ref):
    rp = rpb_ref[...]
    lane = lax.broadcasted_iota(jnp.int32, rp.shape, 1)
    lo = GRID_W - NA_KW
    rolled = pltpu.roll(rp, lo, axis=1)
    first = rp[:, 0:1]
    last = rp[:, 2 * NA_KW - 2:2 * NA_KW - 1]
    ext = jnp.where(lane < lo, first, jnp.where(lane > lo + 2 * NA_KW - 2, last, rolled))

    cq = lax.broadcasted_iota(jnp.int32, (GRID_W, LANES), 0)
    ln = lax.broadcasted_iota(jnp.int32, (GRID_W, LANES), 1)
    ck = jnp.where(ln < GRID_W, ln, ln - GRID_W)
    cs = jnp.clip(cq - NA_KW // 2, 0, GRID_W - NA_KW)
    col_in = (ck >= cs) & (ck < cs + NA_KW)
    low_half = ln < GRID_W
    neg = jnp.full((GRID_W, LANES), NEG_INF, F32)

    def toeplitz(tile, shift):
        dr = tile - 15
        if dr < -(NA_KH - 1) or dr > NA_KH - 1:
            return neg
        row = jnp.broadcast_to(ext[dr + NA_KH - 1:dr + NA_KH, :], (GRID_W, LANES))
        return pltpu.roll(row, shift, axis=1, stride=1, stride_axis=0)

    for copy in range(2):
        for col in range(BIAS_TILES // 2):
            t_lo = 2 * col + copy
            t_hi = t_lo + 1
            a = toeplitz(t_lo, GRID_W + 1)
            b = toeplitz(t_hi, 1)
            pair = jnp.where(low_half, a, b)
            pair = jnp.where(col_in, pair, NEG_INF)
            out_ref[copy, :, col * LANES:(col + 1) * LANES] = pair


def _bias_tables(rpb):
    rp = jnp.pad(rpb, ((0, 0), (0, 0), (0, 1), (0, LANES - (2 * NA_KW - 1))))
    width = BIAS_TILES * GRID_W
    return pl.pallas_call(
        _bias_kernel,
        grid=(DEPTH, NA_HEADS),
        in_specs=[pl.BlockSpec((None, None, 2 * NA_KH, LANES), lambda l, h: (l, h, 0, 0))],
        out_specs=pl.BlockSpec((None, None, 2, GRID_W, width), lambda l, h: (l, h, 0, 0, 0)),
        out_shape=jax.ShapeDtypeStruct((DEPTH, NA_HEADS, 2, GRID_W, width), F32),
        compiler_params=pltpu.CompilerParams(dimension_semantics=("arbitrary", "arbitrary")),
        name="bias_tables",
    )(rp)


def _in_proj(x_ref, mod_ref, row, g1_ref, win_ref, u_ref, n_tok):
    shift1 = mod_ref[pl.ds(row, 1), 0:D_MODEL]
    scale1 = mod_ref[pl.ds(row, 1), D_MODEL:2 * D_MODEL]
    for t0 in range(0, n_tok, TOK_TILE):
        x = x_ref[t0:t0 + TOK_TILE, :]
        h = ((_rms(x) * g1_ref[...]) * (1.0 + scale1) + shift1).astype(BF16)
        u_ref[t0:t0 + TOK_TILE, :] = _dot(h, win_ref[...])


def _softmax_pv(parts):
    m = None
    for s, _ in parts:
        mi = jnp.max(s, axis=-1, keepdims=True)
        m = mi if m is None else jnp.maximum(m, mi)
    den = None
    acc = None
    for s, v in parts:
        e = jnp.exp(s - m)
        li = jnp.sum(e, axis=-1, keepdims=True)
        oi = _dot(e.astype(BF16), v)
        den = li if den is None else den + li
        acc = oi if acc is None else acc + oi
    return acc / den


def _context_attention(u_ref, kv_ref, y_ref, n_tok):
    scale = NA_HEAD_DIM ** -0.5
    lane = lax.broadcasted_iota(jnp.int32, (n_tok, LANES), 1)
    low = lane < NA_HEAD_DIM
    for hp in range(NA_HEADS // 2):
        c0 = hp * LANES
        qpair = u_ref[:, c0:c0 + LANES] * scale
        kpair = kv_ref[:, c0:c0 + LANES]
        vpair = kv_ref[:, NA_WIDTH + c0:NA_WIDTH + c0 + LANES]
        outs = []
        for hh in range(2):
            q = jnp.where(low if hh == 0 else ~low, qpair, 0.0).astype(BF16)
            outs.append(_softmax_pv([(_dot_nt(q, kpair), vpair)]))
        y_ref[:, c0:c0 + LANES] = jnp.where(low, outs[0], outs[1]).astype(BF16)


def _window_start(r):
    return min(max(r - NA_KH // 2, 0), GRID_ROWS - NA_KH)


def _neighbourhood_attention(u_ref, kv_ref, ctx_ref, bias_ref, y_ref):
    scale = NA_HEAD_DIM ** -0.5
    blk_tok = ATT_BLOCK_ROWS * GRID_W
    key_tok = ATT_KEY_ROWS * GRID_W
    lane = lax.broadcasted_iota(jnp.int32, (blk_tok, LANES), 1)
    low = lane < NA_HEAD_DIM
    klane = lax.broadcasted_iota(jnp.int32, (GRID_W, key_tok), 1)
    for blk in range(GRID_ROWS // ATT_BLOCK_ROWS):
        r0 = blk * ATT_BLOCK_ROWS
        j0 = min(_window_start(r0), GRID_ROWS - ATT_KEY_ROWS)
        assert _window_start(r0 + ATT_BLOCK_ROWS - 1) + NA_KH <= j0 + ATT_KEY_ROWS
        q0 = r0 * GRID_W
        k0 = j0 * GRID_W
        for hp in range(NA_HEADS // 2):
            c0 = hp * LANES
            qpair = u_ref[q0:q0 + blk_tok, c0:c0 + LANES] * scale
            kpair = kv_ref[k0:k0 + key_tok, c0:c0 + LANES]
            vpair = kv_ref[k0:k0 + key_tok, NA_WIDTH + c0:NA_WIDTH + c0 + LANES]
            kctx = ctx_ref[:, c0:c0 + LANES]
            vctx = ctx_ref[:, NA_WIDTH + c0:NA_WIDTH + c0 + LANES]
            outs = []
            for hh in range(2):
                head = 2 * hp + hh
                q = jnp.where(low if hh == 0 else ~low, qpair, 0.0).astype(BF16)
                s_loc = _dot_nt(q, kpair)
                s_ctx = _dot_nt(q, kctx)
                strips = []
                for rr in range(ATT_BLOCK_ROWS):
                    r = r0 + rr
                    rs = _window_start(r)
                    tile0 = 15 + j0 - r
                    copy = tile0 % 2
                    b0 = (tile0 - copy) * GRID_W
                    bias = bias_ref[head, copy, :, b0:b0 + key_tok]
                    valid = (klane >= (rs - j0) * GRID_W) & (klane < (rs - j0 + NA_KH) * GRID_W)
                    s = s_loc[rr * GRID_W:(rr + 1) * GRID_W, :] + bias
                    strips.append(jnp.where(valid, s, NEG_INF))
                s_loc = jnp.concatenate(strips, axis=0)
                outs.append(_softmax_pv([(s_loc, vpair), (s_ctx, vctx)]))
            y_ref[q0:q0 + blk_tok, c0:c0 + LANES] = jnp.where(low, outs[0], outs[1]).astype(BF16)


def _pool_mixer(u_ref, pp_ref, wpool_ref, pscale_ref, y_ref, n_tok):
    c_in = 3 * NA_WIDTH
    zeros = jnp.zeros((POOL_PAD, POOL_WIDTH), F32)
    pp_ref[0:POOL_PAD, :] = zeros
    pp_ref[POOL_PAD + n_tok:2 * POOL_PAD + n_tok, :] = zeros
    pp_ref[POOL_PAD:POOL_PAD + n_tok, :] = u_ref[:, c_in:c_in + POOL_WIDTH]
    lane = lax.broadcasted_iota(jnp.int32, (MIX_CHUNK, LANES), 1)
    first = lane < POOL_GROUP_DIM
    tok = lax.broadcasted_iota(jnp.int32, (MIX_CHUNK, LANES), 0)
    wpool = wpool_ref[...].astype(BF16)

    for t0 in range(0, n_tok, MIX_CHUNK):
        def ld(d, half):
            a = POOL_PAD + t0 + d
            return pp_ref[a:a + MIX_CHUNK, half * LANES:(half + 1) * LANES]

        def window(half, lo, hi):
            acc = None
            for d in range(lo, hi):
                acc = ld(d, half) if acc is None else acc + ld(d, half)
            return acc

        a2 = window(0, -1, 1)
        a4 = a2 + ld(-2, 0) + ld(1, 0)
        a8 = window(1, -4, 4)
        a16 = a8 + window(1, -8, -4) + window(1, 4, 8)
        t = tok + t0
        halves = []
        for half, (small, big, ws, wb) in enumerate(((a2, a4, 1, 2), (a8, a16, 4, 8))):
            hw = jnp.where(first, ws, wb)
            cnt = (jnp.minimum(t + hw, n_tok) - jnp.maximum(t - hw, 0)).astype(F32)
            halves.append(jnp.where(first, small, big) / cnt - ld(0, half))
        pooled = jnp.concatenate(halves, axis=1).astype(BF16)
        y = _dot(pooled, wpool) * pscale_ref[...]
        y_ref[t0:t0 + MIX_CHUNK, NA_WIDTH:NA_WIDTH + POOL_WIDTH] = y.astype(BF16)


def _conv_mixer(u_ref, hp_ref, wdw_ref, bdw_ref, cng_ref, cnb_ref, y_ref, n_tok):
    c_a = 3 * NA_WIDTH + POOL_WIDTH
    c_g = c_a + CONV_WIDTH
    zeros = jnp.zeros((CONV_PAD, CONV_WIDTH), F32)
    hp_ref[0:CONV_PAD, :] = zeros
    hp_ref[CONV_PAD + n_tok:2 * CONV_PAD + n_tok, :] = zeros
    for t0 in range(0, n_tok, TOK_TILE):
        a = u_ref[t0:t0 + TOK_TILE, c_a:c_a + CONV_WIDTH]
        g = u_ref[t0:t0 + TOK_TILE, c_g:c_g + CONV_WIDTH]
        hp_ref[CONV_PAD + t0:CONV_PAD + t0 + TOK_TILE, :] = a * _sigmoid(g)
    off = CONV_PAD - CONV_K // 2
    for t0 in range(0, n_tok, MIX_CHUNK):
        acc = None
        for k in range(CONV_K):
            term = wdw_ref[k:k + 1, :] * hp_ref[t0 + off + k:t0 + off + k + MIX_CHUNK, :]
            acc = term if acc is None else acc + term
        acc = acc + bdw_ref[...]
        xc = acc - jnp.mean(acc, axis=-1, keepdims=True)
        y = xc * lax.rsqrt(jnp.mean(xc * xc, axis=-1, keepdims=True) + EPS)
        y = y * cng_ref[...] + cnb_ref[...]
        y_ref[t0:t0 + MIX_CHUNK, NA_WIDTH + POOL_WIDTH:D_MODEL] = (y * _sigmoid(y)).astype(BF16)


def _mix_prompt_kernel(layer_ref, x_ref, mod_ref, g1_ref, win_ref, wpool_ref, pscale_ref, wdw_ref,
                       bdw_ref, cng_ref, cnb_ref,
                       cin_ref, coa_ref, cob_ref, coc_ref, cout_ref, cgu_ref, cdown_ref,
                       y_ref, k_ref, v_ref, wmix_ref,
                       bg_ref, boa_ref, bob_ref, boc_ref, bout_ref, bgu_ref, bdown_ref,
                       u_ref, kv_ref, pp_ref, hp_ref):
    del layer_ref

    @pl.when(pl.program_id(0) == 0)
    def _():
        wmix_ref[...] = win_ref[...].astype(BF16)

    bg_ref[...] = cin_ref[:, N_MIX:].astype(BF16)
    for src, dst in ((coa_ref, boa_ref), (cob_ref, bob_ref), (coc_ref, boc_ref),
                     (cout_ref, bout_ref), (cgu_ref, bgu_ref), (cdown_ref, bdown_ref)):
        dst[...] = src[...].astype(BF16)

    _in_proj(x_ref, mod_ref, 0, g1_ref, wmix_ref, u_ref, SEQ)
    k_ref[...] = u_ref[:, NA_WIDTH:2 * NA_WIDTH]
    v_ref[...] = u_ref[:, 2 * NA_WIDTH:3 * NA_WIDTH]
    kv_ref[...] = u_ref[:, NA_WIDTH:3 * NA_WIDTH].astype(BF16)
    _context_attention(u_ref, kv_ref, y_ref, SEQ)
    _pool_mixer(u_ref, pp_ref, wpool_ref, pscale_ref, y_ref, SEQ)
    _conv_mixer(u_ref, hp_ref, wdw_ref, bdw_ref, cng_ref, cnb_ref, y_ref, SEQ)


def _mix_sample_kernel(layer_ref, x_ref, mod_ref, g1_ref, wmix_ref, wpool_ref, pscale_ref, wdw_ref,
                       bdw_ref, cng_ref, cnb_ref, ck_ref, cv_ref, bias_ref, y_ref,
                       u_ref, kv_ref, ctx_ref, pp_ref, hp_ref):
    del layer_ref
    row = 1 + pl.program_id(0)
    _in_proj(x_ref, mod_ref, row, g1_ref, wmix_ref, u_ref, DEC_SEQ)
    kv_ref[...] = u_ref[:, NA_WIDTH:3 * NA_WIDTH].astype(BF16)
    ctx_ref[:, 0:NA_WIDTH] = ck_ref[...].astype(BF16)
    ctx_ref[:, NA_WIDTH:2 * NA_WIDTH] = cv_ref[...].astype(BF16)
    _neighbourhood_attention(u_ref, kv_ref, ctx_ref, bias_ref, y_ref)
    _pool_mixer(u_ref, pp_ref, wpool_ref, pscale_ref, y_ref, DEC_SEQ)
    _conv_mixer(u_ref, hp_ref, wdw_ref, bdw_ref, cng_ref, cnb_ref, y_ref, DEC_SEQ)


def _small_weight_specs():
    return [_layer_spec((POOL_WIDTH, POOL_WIDTH)), _layer_spec((1, POOL_WIDTH)),
            _layer_spec((CONV_K, CONV_WIDTH)), _layer_spec((1, CONV_WIDTH)),
            _layer_spec((1, CONV_WIDTH)), _layer_spec((1, CONV_WIDTH))]


def _mix_scratch(n_tok):
    return [
        pltpu.VMEM((n_tok, N_MIX), F32),
        pltpu.VMEM((n_tok, 2 * NA_WIDTH), BF16),
    ], [
        pltpu.VMEM((n_tok + 2 * POOL_PAD, POOL_WIDTH), F32),
        pltpu.VMEM((n_tok + 2 * CONV_PAD, CONV_WIDTH), F32),
    ]


def _mix_prompt(layer, x_all, mods, g1, w_in, small, cast_w):
    head, tail = _mix_scratch(SEQ)
    chunk_specs, bf_specs, bf_shapes = [], [], []
    for rows, cols in CAST_WEIGHTS:
        r = rows // BATCH
        chunk_specs.append(pl.BlockSpec((None, r, cols), lambda b, l: (l[0], b, 0)))
        out_cols = N_GATE if cols == N_IN else cols
        bf_specs.append(pl.BlockSpec((r, out_cols), lambda b, l: (b, 0)))
        bf_shapes.append(jax.ShapeDtypeStruct((rows, out_cols), BF16))
    grid_spec = pltpu.PrefetchScalarGridSpec(
        num_scalar_prefetch=1,
        grid=(BATCH,),
        in_specs=[pl.BlockSpec((SEQ, D_MODEL), lambda b, l: (b, 0)),
                  _layer_spec((MOD_ROWS, N_MOD * D_MODEL)),
                  _layer_spec((1, D_MODEL)),
                  _layer_spec((D_MODEL, N_MIX))] + _small_weight_specs() + chunk_specs,
        out_specs=[pl.BlockSpec((SEQ, D_MODEL), lambda b, l: (b, 0)),
                   pl.BlockSpec((None, SEQ, NA_WIDTH), lambda b, l: (b, 0, 0)),
                   pl.BlockSpec((None, SEQ, NA_WIDTH), lambda b, l: (b, 0, 0)),
                   pl.BlockSpec((D_MODEL, N_MIX), lambda b, l: (0, 0))] + bf_specs,
        scratch_shapes=head + tail,
    )
    return pl.pallas_call(
        _mix_prompt_kernel,
        grid_spec=grid_spec,
        out_shape=[jax.ShapeDtypeStruct((N_PROMPT_TOK, D_MODEL), BF16),
                   jax.ShapeDtypeStruct((BATCH, SEQ, NA_WIDTH), F32),
                   jax.ShapeDtypeStruct((BATCH, SEQ, NA_WIDTH), F32),
                   jax.ShapeDtypeStruct((D_MODEL, N_MIX), BF16)] + bf_shapes,
        compiler_params=pltpu.CompilerParams(
            dimension_semantics=("arbitrary",), vmem_limit_bytes=VMEM_LIMIT),
        name="mix_prompt",
    )(layer, x_all, mods, g1, w_in, *small, *cast_w)


def _mix_sample(layer, x_all, mods, g1, wmix, small, cache_k, cache_v, bias):
    head, tail = _mix_scratch(DEC_SEQ)
    first_blk = N_PROMPT_TOK // DEC_SEQ
    ctx_spec = pl.BlockSpec((None, None, PAST_LEN, NA_WIDTH), lambda b, l: (b, l[0], 0, 0))
    grid_spec = pltpu.PrefetchScalarGridSpec(
        num_scalar_prefetch=1,
        grid=(DEC_BATCH,),
        in_specs=[pl.BlockSpec((DEC_SEQ, D_MODEL), lambda b, l: (first_blk + b, 0)),
                  _layer_spec((MOD_ROWS, N_MOD * D_MODEL)),
                  _layer_spec((1, D_MODEL)),
                  _whole_spec((D_MODEL, N_MIX))] + _small_weight_specs()
        + [ctx_spec, ctx_spec, _layer_spec((NA_HEADS, 2, GRID_W, BIAS_TILES * GRID_W))],
        out_specs=pl.BlockSpec((DEC_SEQ, D_MODEL), lambda b, l: (b, 0)),
        scratch_shapes=head + [pltpu.VMEM((PAST_LEN, 2 * NA_WIDTH), BF16)] + tail,
    )
    return pl.pallas_call(
        _mix_sample_kernel,
        grid_spec=grid_spec,
        out_shape=jax.ShapeDtypeStruct((N_SAMPLE_TOK, D_MODEL), BF16),
        compiler_params=pltpu.CompilerParams(
            dimension_semantics=("arbitrary",), vmem_limit_bytes=VMEM_LIMIT),
        name="mix_sample",
    )(layer, x_all, mods, g1, wmix, *small, cache_k, cache_v, bias)


PROMPT_TILES = N_PROMPT_TOK // TOK_TILE
SAMPLE_TILES = N_SAMPLE_TOK // TOK_TILE


def _ffn_kernel(final, layer_ref, x_ref, yp_ref, ys_ref, mod_ref, g1_ref, g2_ref, fg_ref, bg_ref,
                wg_ref, woa_ref, wob_ref, woc_ref, wout_ref, wgu_ref, wdown_ref, o_ref):
    del layer_ref
    i = pl.program_id(0)
    is_prompt = i < PROMPT_TILES
    row = jnp.where(is_prompt, 0, 1 + (i - PROMPT_TILES) // (DEC_SEQ // TOK_TILE))

    def mod(j):
        return mod_ref[pl.ds(row, 1), j * D_MODEL:(j + 1) * D_MODEL]

    x = x_ref[...]
    h = ((_rms(x) * g1_ref[...]) * (1.0 + mod(1)) + mod(0)).astype(BF16)
    branches = ((0, NA_WIDTH, woa_ref), (NA_WIDTH, NA_WIDTH + POOL_WIDTH, wob_ref),
                (NA_WIDTH + POOL_WIDTH, D_MODEL, woc_ref))
    merged = None
    for br, (c0, c1, w_ref) in enumerate(branches):
        g0 = br * D_MODEL
        gate = _sigmoid(_dot(h, wg_ref[:, g0:g0 + D_MODEL]) + bg_ref[:, g0:g0 + D_MODEL])
        y = jnp.where(is_prompt, yp_ref[:, c0:c1], ys_ref[:, c0:c1])
        term = gate * _dot(y, w_ref[...])
        merged = term if merged is None else merged + term
    x1 = x + mod(2) * _dot(merged.astype(BF16), wout_ref[...])
    h2 = ((_rms(x1) * g2_ref[...]) * (1.0 + mod(4)) + mod(3)).astype(BF16)
    acc = None
    for c0, c1 in FFN_CHUNKS:
        a = _dot(h2, wgu_ref[:, c0:c1])
        b = _dot(h2, wgu_ref[:, FFN_HIDDEN + c0:FFN_HIDDEN + c1])
        t = ((a * _sigmoid(a)) * b).astype(BF16)
        d = _dot(t, wdown_ref[c0:c1, :])
        acc = d if acc is None else acc + d
    x2 = x1 + mod(5) * acc
    if final:
        x2 = _rms(x2) * fg_ref[...]
    o_ref[...] = x2


def _ffn(layer, x_all, y_p, y_s, mods, g1, g2, fg, b_gate, weights, final):
    tile = pl.BlockSpec((TOK_TILE, D_MODEL), lambda i, l: (i, 0))
    yp_spec = pl.BlockSpec((TOK_TILE, D_MODEL), lambda i, l: (jnp.minimum(i, PROMPT_TILES - 1), 0))
    ys_spec = pl.BlockSpec((TOK_TILE, D_MODEL), lambda i, l: (jnp.maximum(i - PROMPT_TILES, 0), 0))
    grid_spec = pltpu.PrefetchScalarGridSpec(
        num_scalar_prefetch=1,
        grid=(N_TOK // TOK_TILE,),
        in_specs=[tile, yp_spec, ys_spec,
                  _layer_spec((MOD_ROWS, N_MOD * D_MODEL)),
                  _layer_spec((1, D_MODEL)), _layer_spec((1, D_MODEL)), _whole_spec((1, D_MODEL)),
                  _layer_spec((1, N_GATE)),
                  _whole_spec((D_MODEL, N_GATE)),
                  _whole_spec((NA_WIDTH, D_MODEL)), _whole_spec((POOL_WIDTH, D_MODEL)),
                  _whole_spec((CONV_WIDTH, D_MODEL)), _whole_spec((D_MODEL, D_MODEL)),
                  _whole_spec((D_MODEL, 2 * FFN_HIDDEN)), _whole_spec((FFN_HIDDEN, D_MODEL))],
        out_specs=tile,
    )
    return pl.pallas_call(
        functools.partial(_ffn_kernel, final),
        grid_spec=grid_spec,
        out_shape=jax.ShapeDtypeStruct((N_TOK, D_MODEL), F32),
        compiler_params=pltpu.CompilerParams(
            dimension_semantics=("arbitrary",), vmem_limit_bytes=VMEM_LIMIT),
        name="merge_ffn",
    )(layer, x_all, y_p, y_s, mods, g1, g2, fg, b_gate, *weights)


def _block_diag(w_pool):
    out = jnp.zeros((DEPTH, POOL_WIDTH, POOL_WIDTH), w_pool.dtype)
    for g in range(POOL_GROUPS):
        a = g * POOL_GROUP_DIM
        out = out.at[:, a:a + POOL_GROUP_DIM, a:a + POOL_GROUP_DIM].set(w_pool[:, g])
    return out


def kernel(x_prompt, x_sample, cache_k, cache_v, c, c_ctx, w_mod, b_mod, norm1_g, norm2_g, w_in,
           b_gate, rpb, w_oa, w_pool, pool_scale, w_ob, w_dw, b_dw, conv_norm_g, conv_norm_b,
           w_oc, w_out, w_gu, w_down, final_g):
    x_all = jnp.concatenate([x_prompt.reshape(N_PROMPT_TOK, D_MODEL),
                             x_sample.reshape(N_SAMPLE_TOK, D_MODEL)], axis=0)
    cond = jnp.concatenate([c_ctx[None, :], c,
                            jnp.zeros((MOD_ROWS - 1 - DEC_BATCH, D_MODEL), F32)], axis=0)
    mods = _modulation(cond, w_mod, b_mod)
    bias = _bias_tables(rpb)
    ck = cache_k.reshape(DEC_BATCH, DEPTH, PAST_LEN, NA_WIDTH)
    cv = cache_v.reshape(DEC_BATCH, DEPTH, PAST_LEN, NA_WIDTH)
    g1 = norm1_g.reshape(DEPTH, 1, D_MODEL)
    g2 = norm2_g.reshape(DEPTH, 1, D_MODEL)
    fg = final_g.reshape(1, D_MODEL)
    bg = b_gate.reshape(DEPTH, 1, N_GATE)
    small = (_block_diag(w_pool), pool_scale.reshape(DEPTH, 1, POOL_WIDTH), w_dw,
             b_dw.reshape(DEPTH, 1, CONV_WIDTH), conv_norm_g.reshape(DEPTH, 1, CONV_WIDTH),
             conv_norm_b.reshape(DEPTH, 1, CONV_WIDTH))
    cast_w = (w_in, w_oa, w_ob, w_oc, w_out, w_gu, w_down)
    ks, vs = [], []
    for l in range(DEPTH):
        layer = jnp.full((1,), l, jnp.int32)
        y_p, k_l, v_l, wmix, *ffn_w = _mix_prompt(layer, x_all, mods, g1, w_in, small, cast_w)
        y_s = _mix_sample(layer, x_all, mods, g1, wmix, small, ck, cv, bias)
        ks.append(k_l)
        vs.append(v_l)
        x_all = _ffn(layer, x_all, y_p, y_s, mods, g1, g2, fg, bg, ffn_w, l == DEPTH - 1)
    y_prompt = x_all[:N_PROMPT_TOK].reshape(BATCH, SEQ, D_MODEL)
    y_sample = x_all[N_PROMPT_TOK:].reshape(DEC_BATCH, DEC_SEQ, D_MODEL)
    kv_shape = (BATCH, DEPTH, SEQ, NA_HEADS, NA_HEAD_DIM)
    new_k = jnp.stack(ks, axis=1).reshape(kv_shape)
    new_v = jnp.stack(vs, axis=1).reshape(kv_shape)
    return (y_prompt, y_sample, new_k, new_v)
```

```python
import functools

import jax
import jax.numpy as jnp
from jax import lax
from jax.experimental import pallas as pl
from jax.experimental.pallas import tpu as pltpu

D_MODEL = 1024
BATCH = 16
SEQ = 256
DEPTH = 4
DEC_BATCH = 2
DEC_SEQ = 1024
PAST_LEN = 256
GRID_W = 64
GRID_ROWS = DEC_SEQ // GRID_W
NA_HEADS = 8
NA_HEAD_DIM = 64
NA_WIDTH = NA_HEADS * NA_HEAD_DIM
NA_KH = 8
NA_KW = 16
POOL_WIDTH = 256
POOL_GROUPS = 4
POOL_GROUP_DIM = 64
CONV_WIDTH = 256
CONV_K = 31
N_BRANCH = 3
N_MIX = 3 * NA_WIDTH + POOL_WIDTH + 2 * CONV_WIDTH
N_GATE = N_BRANCH * D_MODEL
N_IN = N_MIX + N_GATE
FFN_HIDDEN = 2816
N_MOD = 6
EPS = 1e-6
NEG_INF = -1e30

N_PROMPT_TOK = BATCH * SEQ
N_SAMPLE_TOK = DEC_BATCH * DEC_SEQ
N_TOK = N_PROMPT_TOK + N_SAMPLE_TOK

LANES = 128
SUBLANES = 8
MOD_ROWS = 8
MOD_BLOCK_N = 1536
TOK_TILE = 256
ATT_BLOCK_ROWS = 4
ATT_KEY_ROWS = 12
BIAS_TILES = 32
POOL_PAD = 8
CONV_PAD = 16
MIX_CHUNK = 128
FFN_CHUNKS = ((0, 1024), (1024, 2048), (2048, 2816))
VMEM_LIMIT = 56 * 1024 * 1024

F32 = jnp.float32
BF16 = jnp.bfloat16

CAST_WEIGHTS = ((D_MODEL, N_IN), (NA_WIDTH, D_MODEL), (POOL_WIDTH, D_MODEL), (CONV_WIDTH, D_MODEL),
                (D_MODEL, D_MODEL), (D_MODEL, 2 * FFN_HIDDEN), (FFN_HIDDEN, D_MODEL))


def _sigmoid(x):
    return 1.0 / (1.0 + jnp.exp(-x))


def _rms(x):
    return x * lax.rsqrt(jnp.mean(x * x, axis=-1, keepdims=True) + EPS)


def _dot(a, b):
    return jnp.dot(a, b, preferred_element_type=F32)


def _dot_nt(a, b):
    return lax.dot_general(a, b, (((1,), (1,)), ((), ())), preferred_element_type=F32)


def _layer_spec(shape):
    zeros = (0,) * len(shape)
    return pl.BlockSpec((None,) + tuple(shape), lambda i, l: (l[0],) + zeros,
                        pipeline_mode=pl.Buffered(1))


def _whole_spec(shape):
    zeros = (0,) * len(shape)
    return pl.BlockSpec(tuple(shape), lambda i, l: zeros, pipeline_mode=pl.Buffered(1))


def _mod_kernel(cond_ref, w_ref, b_ref, out_ref):
    c = cond_ref[...]
    s = (c * _sigmoid(c)).astype(BF16)
    out_ref[...] = _dot(s, w_ref[...].astype(BF16)) + b_ref[...]


def _modulation(cond, w_mod, b_mod):
    n = N_MOD * D_MODEL
    return pl.pallas_call(
        _mod_kernel,
        grid=(DEPTH, n // MOD_BLOCK_N),
        in_specs=[
            pl.BlockSpec((MOD_ROWS, D_MODEL), lambda l, j: (0, 0)),
            pl.BlockSpec((None, D_MODEL, MOD_BLOCK_N), lambda l, j: (l, 0, j)),
            pl.BlockSpec((None, 1, MOD_BLOCK_N), lambda l, j: (l, 0, j)),
        ],
        out_specs=pl.BlockSpec((None, MOD_ROWS, MOD_BLOCK_N), lambda l, j: (l, 0, j)),
        out_shape=jax.ShapeDtypeStruct((DEPTH, MOD_ROWS, n), F32),
        compiler_params=pltpu.CompilerParams(
            dimension_semantics=("arbitrary", "arbitrary"), vmem_limit_bytes=VMEM_LIMIT),
        name="modulation",
    )(cond, w_mod, b_mod.reshape(DEPTH, 1, n))


def _bias_kernel(rpb_ref, out_ref):
    cq = lax.broadcasted_iota(jnp.int32, (GRID_W, LANES), 0)
    ln = lax.broadcasted_iota(jnp.int32, (GRID_W, LANES), 1)
    ck = jnp.where(ln < GRID_W, ln, ln - GRID_W)
    cs = jnp.clip(cq - NA_KW // 2, 0, GRID_W - NA_KW)
    col_in = (ck >= cs) & (ck < cs + NA_KW)
    low_half = ln < GRID_W
    neg = jnp.full((GRID_W, LANES), NEG_INF, F32)
    lo = GRID_W - NA_KW

    for head in range(NA_HEADS):
        rp = rpb_ref[head]
        lane = lax.broadcasted_iota(jnp.int32, rp.shape, 1)
        rolled = pltpu.roll(rp, lo, axis=1)
        first = rp[:, 0:1]
        last = rp[:, 2 * NA_KW - 2:2 * NA_KW - 1]
        ext = jnp.where(lane < lo, first, jnp.where(lane > lo + 2 * NA_KW - 2, last, rolled))

        def toeplitz(tile, shift):
            dr = tile - 15
            if dr < -(NA_KH - 1) or dr > NA_KH - 1:
                return neg
            row = jnp.broadcast_to(ext[dr + NA_KH - 1:dr + NA_KH, :], (GRID_W, LANES))
            return pltpu.roll(row, shift, axis=1, stride=1, stride_axis=0)

        for copy in range(2):
            for col in range(BIAS_TILES // 2):
                t_lo = 2 * col + copy
                t_hi = t_lo + 1
                a = toeplitz(t_lo, GRID_W + 1)
                b = toeplitz(t_hi, 1)
                pair = jnp.where(low_half, a, b)
                pair = jnp.where(col_in, pair, NEG_INF)
                out_ref[head, copy, :, col * LANES:(col + 1) * LANES] = pair


def _bias_tables(rpb):
    rp = jnp.pad(rpb, ((0, 0), (0, 0), (0, 1), (0, LANES - (2 * NA_KW - 1))))
    width = BIAS_TILES * GRID_W
    return pl.pallas_call(
        _bias_kernel,
        grid=(DEPTH,),
        in_specs=[pl.BlockSpec((None, NA_HEADS, 2 * NA_KH, LANES), lambda l: (l, 0, 0, 0))],
        out_specs=pl.BlockSpec((None, NA_HEADS, 2, GRID_W, width), lambda l: (l, 0, 0, 0, 0)),
        out_shape=jax.ShapeDtypeStruct((DEPTH, NA_HEADS, 2, GRID_W, width), F32),
        compiler_params=pltpu.CompilerParams(
            dimension_semantics=("arbitrary",), vmem_limit_bytes=VMEM_LIMIT),
        name="bias_tables",
    )(rp)


def _in_proj(x_ref, mod_ref, row, g1_ref, win_ref, u_ref, n_tok):
    shift1 = mod_ref[pl.ds(row, 1), 0:D_MODEL]
    scale1 = mod_ref[pl.ds(row, 1), D_MODEL:2 * D_MODEL]
    for t0 in range(0, n_tok, TOK_TILE):
        x = x_ref[t0:t0 + TOK_TILE, :]
        h = ((_rms(x) * g1_ref[...]) * (1.0 + scale1) + shift1).astype(BF16)
        u_ref[t0:t0 + TOK_TILE, :] = _dot(h, win_ref[...])


def _softmax_pv(parts):
    m = None
    for s, _ in parts:
        mi = jnp.max(s, axis=-1, keepdims=True)
        m = mi if m is None else jnp.maximum(m, mi)
    den = None
    acc = None
    for s, v in parts:
        e = jnp.exp(s - m)
        li = jnp.sum(e, axis=-1, keepdims=True)
        oi = _dot(e.astype(BF16), v)
        den = li if den is None else den + li
        acc = oi if acc is None else acc + oi
    return acc / den


def _context_attention(u_ref, kv_ref, y_ref, n_tok):
    scale = NA_HEAD_DIM ** -0.5
    lane = lax.broadcasted_iota(jnp.int32, (n_tok, LANES), 1)
    low = lane < NA_HEAD_DIM
    for hp in range(NA_HEADS // 2):
        c0 = hp * LANES
        qpair = u_ref[:, c0:c0 + LANES] * scale
        kpair = kv_ref[:, c0:c0 + LANES]
        vpair = kv_ref[:, NA_WIDTH + c0:NA_WIDTH + c0 + LANES]
        outs = []
        for hh in range(2):
            q = jnp.where(low if hh == 0 else ~low, qpair, 0.0).astype(BF16)
            outs.append(_softmax_pv([(_dot_nt(q, kpair), vpair)]))
        y_ref[:, c0:c0 + LANES] = jnp.where(low, outs[0], outs[1]).astype(BF16)


def _window_start(r):
    return min(max(r - NA_KH // 2, 0), GRID_ROWS - NA_KH)


def _neighbourhood_attention(u_ref, kv_ref, ctx_ref, bias_ref, y_ref):
    scale = NA_HEAD_DIM ** -0.5
    blk_tok = ATT_BLOCK_ROWS * GRID_W
    key_tok = ATT_KEY_ROWS * GRID_W
    lane = lax.broadcasted_iota(jnp.int32, (blk_tok, LANES), 1)
    low = lane < NA_HEAD_DIM
    klane = lax.broadcasted_iota(jnp.int32, (GRID_W, key_tok), 1)
    for blk in range(GRID_ROWS // ATT_BLOCK_ROWS):
        r0 = blk * ATT_BLOCK_ROWS
        j0 = min(_window_start(r0), GRID_ROWS - ATT_KEY_ROWS)
        assert _window_start(r0 + ATT_BLOCK_ROWS - 1) + NA_KH <= j0 + ATT_KEY_ROWS
        q0 = r0 * GRID_W
        k0 = j0 * GRID_W
        for hp in range(NA_HEADS // 2):
            c0 = hp * LANES
            qpair = u_ref[q0:q0 + blk_tok, c0:c0 + LANES] * scale
            kpair = kv_ref[k0:k0 + key_tok, c0:c0 + LANES]
            vpair = kv_ref[k0:k0 + key_tok, NA_WIDTH + c0:NA_WIDTH + c0 + LANES]
            kctx = ctx_ref[:, c0:c0 + LANES]
            vctx = ctx_ref[:, NA_WIDTH + c0:NA_WIDTH + c0 + LANES]
            outs = []
            for hh in range(2):
                head = 2 * hp + hh
                q = jnp.where(low if hh == 0 else ~low, qpair, 0.0).astype(BF16)
                s_loc = _dot_nt(q, kpair)
                s_ctx = _dot_nt(q, kctx)
                strips = []
                for rr in range(ATT_BLOCK_ROWS):
                    r = r0 + rr
                    rs = _window_start(r)
                    tile0 = 15 + j0 - r
                    copy = tile0 % 2
                    b0 = (tile0 - copy) * GRID_W
                    bias = bias_ref[head, copy, :, b0:b0 + key_tok]
                    valid = (klane >= (rs - j0) * GRID_W) & (klane < (rs - j0 + NA_KH) * GRID_W)
                    s = s_loc[rr * GRID_W:(rr + 1) * GRID_W, :] + bias
                    strips.append(jnp.where(valid, s, NEG_INF))
                s_loc = jnp.concatenate(strips, axis=0)
                outs.append(_softmax_pv([(s_loc, vpair), (s_ctx, vctx)]))
            y_ref[q0:q0 + blk_tok, c0:c0 + LANES] = jnp.where(low, outs[0], outs[1]).astype(BF16)


def _pool_mixer(u_ref, pp_ref, wpool_ref, pscale_ref, y_ref, n_tok):
    c_in = 3 * NA_WIDTH
    zeros = jnp.zeros((POOL_PAD, POOL_WIDTH), F32)
    pp_ref[0:POOL_PAD, :] = zeros
    pp_ref[POOL_PAD + n_tok:2 * POOL_PAD + n_tok, :] = zeros
    pp_ref[POOL_PAD:POOL_PAD + n_tok, :] = u_ref[:, c_in:c_in + POOL_WIDTH]
    lane = lax.broadcasted_iota(jnp.int32, (MIX_CHUNK, LANES), 1)
    first = lane < POOL_GROUP_DIM
    tok = lax.broadcasted_iota(jnp.int32, (MIX_CHUNK, LANES), 0)
    wpool = wpool_ref[...].astype(BF16)

    for t0 in range(0, n_tok, MIX_CHUNK):
        def ld(d, half):
            a = POOL_PAD + t0 + d
            return pp_ref[a:a + MIX_CHUNK, half * LANES:(half + 1) * LANES]

        def window(half, lo, hi):
            acc = None
            for d in range(lo, hi):
                acc = ld(d, half) if acc is None else acc + ld(d, half)
            return acc

        a2 = window(0, -1, 1)
        a4 = a2 + ld(-2, 0) + ld(1, 0)
        a8 = window(1, -4, 4)
        a16 = a8 + window(1, -8, -4) + window(1, 4, 8)
        t = tok + t0
        halves = []
        for half, (small, big, ws, wb) in enumerate(((a2, a4, 1, 2), (a8, a16, 4, 8))):
            hw = jnp.where(first, ws, wb)
            cnt = (jnp.minimum(t + hw, n_tok) - jnp.maximum(t - hw, 0)).astype(F32)
            halves.append(jnp.where(first, small, big) / cnt - ld(0, half))
        pooled = jnp.concatenate(halves, axis=1).astype(BF16)
        y = _dot(pooled, wpool) * pscale_ref[...]
        y_ref[t0:t0 + MIX_CHUNK, NA_WIDTH:NA_WIDTH + POOL_WIDTH] = y.astype(BF16)


def _conv_mixer(u_ref, hp_ref, sh_ref, wdw_ref, bdw_ref, cng_ref, cnb_ref, y_ref, n_tok):
    c_a = 3 * NA_WIDTH + POOL_WIDTH
    c_g = c_a + CONV_WIDTH
    zeros = jnp.zeros((CONV_PAD, CONV_WIDTH), F32)
    hp_ref[0:CONV_PAD, :] = zeros
    hp_ref[CONV_PAD + n_tok:2 * CONV_PAD + n_tok, :] = zeros
    for t0 in range(0, n_tok, TOK_TILE):
        a = u_ref[t0:t0 + TOK_TILE, c_a:c_a + CONV_WIDTH]
        g = u_ref[t0:t0 + TOK_TILE, c_g:c_g + CONV_WIDTH]
        hp_ref[CONV_PAD + t0:CONV_PAD + t0 + TOK_TILE, :] = a * _sigmoid(g)
    off = CONV_PAD - CONV_K // 2
    span = MIX_CHUNK + (off + CONV_K - 1) // SUBLANES * SUBLANES
    for t0 in range(0, n_tok, MIX_CHUNK):
        for s in range(1, SUBLANES):
            sh_ref[s, 0:span, :] = hp_ref[t0 + s:t0 + s + span, :]
        acc = None
        for k in range(CONV_K):
            a, s = divmod(off + k, SUBLANES)
            if s == 0:
                win = hp_ref[t0 + a * SUBLANES:t0 + a * SUBLANES + MIX_CHUNK, :]
            else:
                win = sh_ref[s, a * SUBLANES:a * SUBLANES + MIX_CHUNK, :]
            term = wdw_ref[k:k + 1, :] * win
            acc = term if acc is None else acc + term
        acc = acc + bdw_ref[...]
        xc = acc - jnp.mean(acc, axis=-1, keepdims=True)
        y = xc * lax.rsqrt(jnp.mean(xc * xc, axis=-1, keepdims=True) + EPS)
        y = y * cng_ref[...] + cnb_ref[...]
        y_ref[t0:t0 + MIX_CHUNK, NA_WIDTH + POOL_WIDTH:D_MODEL] = (y * _sigmoid(y)).astype(BF16)


def _mix_prompt_kernel(layer_ref, x_ref, mod_ref, g1_ref, win_ref, wpool_ref, pscale_ref, wdw_ref,
                       bdw_ref, cng_ref, cnb_ref,
                       cin_ref, coa_ref, cob_ref, coc_ref, cout_ref, cgu_ref, cdown_ref,
                       kbuf_ref, vbuf_ref, y_ref, k_ref, v_ref, wmix_ref,
                       bg_ref, boa_ref, bob_ref, boc_ref, bout_ref, bgu_ref, bdown_ref,
                       u_ref, kv_ref, pp_ref, hp_ref, sh_ref):
    del layer_ref, kbuf_ref, vbuf_ref

    @pl.when(pl.program_id(0) == 0)
    def _():
        wmix_ref[...] = win_ref[...].astype(BF16)

    bg_ref[...] = cin_ref[:, N_MIX:].astype(BF16)
    for src, dst in ((coa_ref, boa_ref), (cob_ref, bob_ref), (coc_ref, boc_ref),
                     (cout_ref, bout_ref), (cgu_ref, bgu_ref), (cdown_ref, bdown_ref)):
        dst[...] = src[...].astype(BF16)

    _in_proj(x_ref, mod_ref, 0, g1_ref, wmix_ref, u_ref, SEQ)
    k_ref[...] = u_ref[:, NA_WIDTH:2 * NA_WIDTH]
    v_ref[...] = u_ref[:, 2 * NA_WIDTH:3 * NA_WIDTH]
    kv_ref[...] = u_ref[:, NA_WIDTH:3 * NA_WIDTH].astype(BF16)
    _context_attention(u_ref, kv_ref, y_ref, SEQ)
    _pool_mixer(u_ref, pp_ref, wpool_ref, pscale_ref, y_ref, SEQ)
    _conv_mixer(u_ref, hp_ref, sh_ref, wdw_ref, bdw_ref, cng_ref, cnb_ref, y_ref, SEQ)


def _mix_sample_kernel(layer_ref, x_ref, mod_ref, g1_ref, wmix_ref, wpool_ref, pscale_ref, wdw_ref,
                       bdw_ref, cng_ref, cnb_ref, ck_ref, cv_ref, bias_ref, y_ref,
                       u_ref, kv_ref, ctx_ref, pp_ref, hp_ref, sh_ref):
    del layer_ref
    row = 1 + pl.program_id(0)
    _in_proj(x_ref, mod_ref, row, g1_ref, wmix_ref, u_ref, DEC_SEQ)
    kv_ref[...] = u_ref[:, NA_WIDTH:3 * NA_WIDTH].astype(BF16)
    ctx_ref[:, 0:NA_WIDTH] = ck_ref[...].astype(BF16)
    ctx_ref[:, NA_WIDTH:2 * NA_WIDTH] = cv_ref[...].astype(BF16)
    _neighbourhood_attention(u_ref, kv_ref, ctx_ref, bias_ref, y_ref)
    _pool_mixer(u_ref, pp_ref, wpool_ref, pscale_ref, y_ref, DEC_SEQ)
    _conv_mixer(u_ref, hp_ref, sh_ref, wdw_ref, bdw_ref, cng_ref, cnb_ref, y_ref, DEC_SEQ)


def _small_weight_specs():
    return [_layer_spec((POOL_WIDTH, POOL_WIDTH)), _layer_spec((1, POOL_WIDTH)),
            _layer_spec((CONV_K, CONV_WIDTH)), _layer_spec((1, CONV_WIDTH)),
            _layer_spec((1, CONV_WIDTH)), _layer_spec((1, CONV_WIDTH))]


def _mix_scratch(n_tok):
    return [
        pltpu.VMEM((n_tok, N_MIX), F32),
        pltpu.VMEM((n_tok, 2 * NA_WIDTH), BF16),
    ], [
        pltpu.VMEM((n_tok + 2 * POOL_PAD, POOL_WIDTH), F32),
        pltpu.VMEM((n_tok + 2 * CONV_PAD, CONV_WIDTH), F32),
        pltpu.VMEM((SUBLANES, MIX_CHUNK + 2 * CONV_PAD, CONV_WIDTH), F32),
    ]


def _mix_prompt(layer, x_all, mods, g1, w_in, small, cast_w, kbuf, vbuf):
    head, tail = _mix_scratch(SEQ)
    operands = (layer, x_all, mods, g1, w_in, *small, *cast_w, kbuf, vbuf)
    n_in = len(operands)
    chunk_specs, bf_specs, bf_shapes = [], [], []
    for rows, cols in CAST_WEIGHTS:
        r = rows // BATCH
        chunk_specs.append(pl.BlockSpec((None, r, cols), lambda b, l: (l[0], b, 0)))
        out_cols = N_GATE if cols == N_IN else cols
        bf_specs.append(pl.BlockSpec((r, out_cols), lambda b, l: (b, 0)))
        bf_shapes.append(jax.ShapeDtypeStruct((rows, out_cols), BF16))
    grid_spec = pltpu.PrefetchScalarGridSpec(
        num_scalar_prefetch=1,
        grid=(BATCH,),
        in_specs=[pl.BlockSpec((SEQ, D_MODEL), lambda b, l: (b, 0)),
                  _layer_spec((MOD_ROWS, N_MOD * D_MODEL)),
                  _layer_spec((1, D_MODEL)),
                  _layer_spec((D_MODEL, N_MIX))] + _small_weight_specs() + chunk_specs
        + [pl.BlockSpec(memory_space=pl.ANY), pl.BlockSpec(memory_space=pl.ANY)],
        out_specs=[pl.BlockSpec((SEQ, D_MODEL), lambda b, l: (b, 0)),
                   pl.BlockSpec((None, None, SEQ, NA_WIDTH), lambda b, l: (b, l[0], 0, 0)),
                   pl.BlockSpec((None, None, SEQ, NA_WIDTH), lambda b, l: (b, l[0], 0, 0)),
                   pl.BlockSpec((D_MODEL, N_MIX), lambda b, l: (0, 0))] + bf_specs,
        scratch_shapes=head + tail,
    )
    return pl.pallas_call(
        _mix_prompt_kernel,
        grid_spec=grid_spec,
        out_shape=[jax.ShapeDtypeStruct((N_PROMPT_TOK, D_MODEL), BF16),
                   jax.ShapeDtypeStruct(kbuf.shape, F32),
                   jax.ShapeDtypeStruct(vbuf.shape, F32),
                   jax.ShapeDtypeStruct((D_MODEL, N_MIX), BF16)] + bf_shapes,
        input_output_aliases={n_in - 2: 1, n_in - 1: 2},
        compiler_params=pltpu.CompilerParams(
            dimension_semantics=("arbitrary",), vmem_limit_bytes=VMEM_LIMIT),
        name="mix_prompt",
    )(*operands)


def _mix_sample(layer, x_all, mods, g1, wmix, small, cache_k, cache_v, bias):
    head, tail = _mix_scratch(DEC_SEQ)
    first_blk = N_PROMPT_TOK // DEC_SEQ
    ctx_spec = pl.BlockSpec((None, None, PAST_LEN, NA_WIDTH), lambda b, l: (b, l[0], 0, 0))
    grid_spec = pltpu.PrefetchScalarGridSpec(
        num_scalar_prefetch=1,
        grid=(DEC_BATCH,),
        in_specs=[pl.BlockSpec((DEC_SEQ, D_MODEL), lambda b, l: (first_blk + b, 0)),
                  _layer_spec((MOD_ROWS, N_MOD * D_MODEL)),
                  _layer_spec((1, D_MODEL)),
                  _whole_spec((D_MODEL, N_MIX))] + _small_weight_specs()
        + [ctx_spec, ctx_spec, _layer_spec((NA_HEADS, 2, GRID_W, BIAS_TILES * GRID_W))],
        out_specs=pl.BlockSpec((DEC_SEQ, D_MODEL), lambda b, l: (b, 0)),
        scratch_shapes=head + [pltpu.VMEM((PAST_LEN, 2 * NA_WIDTH), BF16)] + tail,
    )
    return pl.pallas_call(
        _mix_sample_kernel,
        grid_spec=grid_spec,
        out_shape=jax.ShapeDtypeStruct((N_SAMPLE_TOK, D_MODEL), BF16),
        compiler_params=pltpu.CompilerParams(
            dimension_semantics=("arbitrary",), vmem_limit_bytes=VMEM_LIMIT),
        name="mix_sample",
    )(layer, x_all, mods, g1, wmix, *small, cache_k, cache_v, bias)


PROMPT_TILES = N_PROMPT_TOK // TOK_TILE
SAMPLE_TILES = N_SAMPLE_TOK // TOK_TILE


def _ffn_kernel(final, layer_ref, x_ref, yp_ref, ys_ref, mod_ref, g1_ref, g2_ref, fg_ref, bg_ref,
                wg_ref, woa_ref, wob_ref, woc_ref, wout_ref, wgu_ref, wdown_ref, o_ref):
    del layer_ref
    i = pl.program_id(0)
    is_prompt = i < PROMPT_TILES
    row = jnp.where(is_prompt, 0, 1 + (i - PROMPT_TILES) // (DEC_SEQ // TOK_TILE))

    def mod(j):
        return mod_ref[pl.ds(row, 1), j * D_MODEL:(j + 1) * D_MODEL]

    x = x_ref[...]
    h = ((_rms(x) * g1_ref[...]) * (1.0 + mod(1)) + mod(0)).astype(BF16)
    branches = ((0, NA_WIDTH, woa_ref), (NA_WIDTH, NA_WIDTH + POOL_WIDTH, wob_ref),
                (NA_WIDTH + POOL_WIDTH, D_MODEL, woc_ref))
    merged = None
    for br, (c0, c1, w_ref) in enumerate(branches):
        g0 = br * D_MODEL
        gate = _sigmoid(_dot(h, wg_ref[:, g0:g0 + D_MODEL]) + bg_ref[:, g0:g0 + D_MODEL])
        y = jnp.where(is_prompt, yp_ref[:, c0:c1], ys_ref[:, c0:c1])
        term = gate * _dot(y, w_ref[...])
        merged = term if merged is None else merged + term
    x1 = x + mod(2) * _dot(merged.astype(BF16), wout_ref[...])
    h2 = ((_rms(x1) * g2_ref[...]) * (1.0 + mod(4)) + mod(3)).astype(BF16)
    acc = None
    for c0, c1 in FFN_CHUNKS:
        a = _dot(h2, wgu_ref[:, c0:c1])
        b = _dot(h2, wgu_ref[:, FFN_HIDDEN + c0:FFN_HIDDEN + c1])
        t = ((a * _sigmoid(a)) * b).astype(BF16)
        d = _dot(t, wdown_ref[c0:c1, :])
        acc = d if acc is None else acc + d
    x2 = x1 + mod(5) * acc
    if final:
        x2 = _rms(x2) * fg_ref[...]
    o_ref[...] = x2


def _ffn(layer, x_all, y_p, y_s, mods, g1, g2, fg, b_gate, weights, final):
    tile = pl.BlockSpec((TOK_TILE, D_MODEL), lambda i, l: (i, 0))
    yp_spec = pl.BlockSpec((TOK_TILE, D_MODEL), lambda i, l: (jnp.minimum(i, PROMPT_TILES - 1), 0))
    ys_spec = pl.BlockSpec((TOK_TILE, D_MODEL), lambda i, l: (jnp.maximum(i - PROMPT_TILES, 0), 0))
    grid_spec = pltpu.PrefetchScalarGridSpec(
        num_scalar_prefetch=1,
        grid=(N_TOK // TOK_TILE,),
        in_specs=[tile, yp_spec, ys_spec,
                  _layer_spec((MOD_ROWS, N_MOD * D_MODEL)),
                  _layer_spec((1, D_MODEL)), _layer_spec((1, D_MODEL)), _whole_spec((1, D_MODEL)),
                  _layer_spec((1, N_GATE)),
                  _whole_spec((D_MODEL, N_GATE)),
                  _whole_spec((NA_WIDTH, D_MODEL)), _whole_spec((POOL_WIDTH, D_MODEL)),
                  _whole_spec((CONV_WIDTH, D_MODEL)), _whole_spec((D_MODEL, D_MODEL)),
                  _whole_spec((D_MODEL, 2 * FFN_HIDDEN)), _whole_spec((FFN_HIDDEN, D_MODEL))],
        out_specs=tile,
    )
    return pl.pallas_call(
        functools.partial(_ffn_kernel, final),
        grid_spec=grid_spec,
        out_shape=jax.ShapeDtypeStruct((N_TOK, D_MODEL), F32),
        compiler_params=pltpu.CompilerParams(
            dimension_semantics=("arbitrary",), vmem_limit_bytes=VMEM_LIMIT),
        name="merge_ffn",
    )(layer, x_all, y_p, y_s, mods, g1, g2, fg, b_gate, *weights)


def _block_diag(w_pool):
    out = jnp.zeros((DEPTH, POOL_WIDTH, POOL_WIDTH), w_pool.dtype)
    for g in range(POOL_GROUPS):
        a = g * POOL_GROUP_DIM
        out = out.at[:, a:a + POOL_GROUP_DIM, a:a + POOL_GROUP_DIM].set(w_pool[:, g])
    return out


def kernel(x_prompt, x_sample, cache_k, cache_v, c, c_ctx, w_mod, b_mod, norm1_g, norm2_g, w_in,
           b_gate, rpb, w_oa, w_pool, pool_scale, w_ob, w_dw, b_dw, conv_norm_g, conv_norm_b,
           w_oc, w_out, w_gu, w_down, final_g):
    x_all = jnp.concatenate([x_prompt.reshape(N_PROMPT_TOK, D_MODEL),
                             x_sample.reshape(N_SAMPLE_TOK, D_MODEL)], axis=0)
    cond = jnp.concatenate([c_ctx[None, :], c,
                            jnp.zeros((MOD_ROWS - 1 - DEC_BATCH, D_MODEL), F32)], axis=0)
    mods = _modulation(cond, w_mod, b_mod)
    bias = _bias_tables(rpb)
    ck = cache_k.reshape(DEC_BATCH, DEPTH, PAST_LEN, NA_WIDTH)
    cv = cache_v.reshape(DEC_BATCH, DEPTH, PAST_LEN, NA_WIDTH)
    g1 = norm1_g.reshape(DEPTH, 1, D_MODEL)
    g2 = norm2_g.reshape(DEPTH, 1, D_MODEL)
    fg = final_g.reshape(1, D_MODEL)
    bg = b_gate.reshape(DEPTH, 1, N_GATE)
    small = (_block_diag(w_pool), pool_scale.reshape(DEPTH, 1, POOL_WIDTH), w_dw,
             b_dw.reshape(DEPTH, 1, CONV_WIDTH), conv_norm_g.reshape(DEPTH, 1, CONV_WIDTH),
             conv_norm_b.reshape(DEPTH, 1, CONV_WIDTH))
    cast_w = (w_in, w_oa, w_ob, w_oc, w_out, w_gu, w_down)
    kbuf = jnp.zeros((BATCH, DEPTH, SEQ, NA_WIDTH), F32)
    vbuf = jnp.zeros((BATCH, DEPTH, SEQ, NA_WIDTH), F32)
    for l in range(DEPTH):
        layer = jnp.full((1,), l, jnp.int32)
        y_p, kbuf, vbuf, wmix, *ffn_w = _mix_prompt(layer, x_all, mods, g1, w_in, small, cast_w,
                                                    kbuf, vbuf)
        y_s = _mix_sample(layer, x_all, mods, g1, wmix, small, ck, cv, bias)
        x_all = _ffn(layer, x_all, y_p, y_s, mods, g1, g2, fg, bg, ffn_w, l == DEPTH - 1)
    y_prompt = x_all[:N_PROMPT_TOK].reshape(BATCH, SEQ, D_MODEL)
    y_sample = x_all[N_PROMPT_TOK:].reshape(DEC_BATCH, DEC_SEQ, D_MODEL)
    kv_shape = (BATCH, DEPTH, SEQ, NA_HEADS, NA_HEAD_DIM)
    new_k = kbuf.reshape(kv_shape)
    new_v = vbuf.reshape(kv_shape)
    return (y_prompt, y_sample, new_k, new_v)
```

```python
import functools

import jax
import jax.numpy as jnp
from jax import lax
from jax.experimental import pallas as pl
from jax.experimental.pallas import tpu as pltpu

D_MODEL = 1024
BATCH = 16
SEQ = 256
DEPTH = 4
DEC_BATCH = 2
DEC_SEQ = 1024
PAST_LEN = 256
GRID_W = 64
GRID_ROWS = DEC_SEQ // GRID_W
NA_HEADS = 8
NA_HEAD_DIM = 64
NA_WIDTH = NA_HEADS * NA_HEAD_DIM
NA_KH = 8
NA_KW = 16
POOL_WIDTH = 256
POOL_GROUPS = 4
POOL_GROUP_DIM = 64
CONV_WIDTH = 256
CONV_K = 31
N_BRANCH = 3
N_MIX = 3 * NA_WIDTH + POOL_WIDTH + 2 * CONV_WIDTH
N_GATE = N_BRANCH * D_MODEL
N_IN = N_MIX + N_GATE
FFN_HIDDEN = 2816
N_MOD = 6
EPS = 1e-6
NEG_INF = -1e30

N_PROMPT_TOK = BATCH * SEQ
N_SAMPLE_TOK = DEC_BATCH * DEC_SEQ
N_TOK = N_PROMPT_TOK + N_SAMPLE_TOK

LANES = 128
SUBLANES = 8
MOD_ROWS = 8
MOD_BLOCK_N = 1536
TOK_TILE = 256
ATT_BLOCK_ROWS = 4
ATT_KEY_ROWS = 12
BIAS_TILES = 32
POOL_PAD = 8
CONV_PAD = 16
MIX_CHUNK = 128
FFN_CHUNKS = ((0, 1024), (1024, 2048), (2048, 2816))
VMEM_LIMIT = 56 * 1024 * 1024

F32 = jnp.float32
BF16 = jnp.bfloat16

CAST_WEIGHTS = ((D_MODEL, N_IN), (NA_WIDTH, D_MODEL), (POOL_WIDTH, D_MODEL), (CONV_WIDTH, D_MODEL),
                (D_MODEL, D_MODEL), (D_MODEL, 2 * FFN_HIDDEN), (FFN_HIDDEN, D_MODEL))


def _sigmoid(x):
    return 1.0 / (1.0 + jnp.exp(-x))


def _rms(x):
    return x * lax.rsqrt(jnp.mean(x * x, axis=-1, keepdims=True) + EPS)


def _dot(a, b):
    return jnp.dot(a, b, preferred_element_type=F32)


def _dot_nt(a, b):
    return lax.dot_general(a, b, (((1,), (1,)), ((), ())), preferred_element_type=F32)


def _layer_spec(shape):
    zeros = (0,) * len(shape)
    return pl.BlockSpec((None,) + tuple(shape), lambda i, l: (l[0],) + zeros,
                        pipeline_mode=pl.Buffered(1))


def _whole_spec(shape):
    zeros = (0,) * len(shape)
    return pl.BlockSpec(tuple(shape), lambda i, l: zeros, pipeline_mode=pl.Buffered(1))


def _mod_kernel(cond_ref, w_ref, b_ref, out_ref):
    c = cond_ref[...]
    s = (c * _sigmoid(c)).astype(BF16)
    out_ref[...] = _dot(s, w_ref[...].astype(BF16)) + b_ref[...]


def _modulation(cond, w_mod, b_mod):
    n = N_MOD * D_MODEL
    return pl.pallas_call(
        _mod_kernel,
        grid=(DEPTH, n // MOD_BLOCK_N),
        in_specs=[
            pl.BlockSpec((MOD_ROWS, D_MODEL), lambda l, j: (0, 0)),
            pl.BlockSpec((None, D_MODEL, MOD_BLOCK_N), lambda l, j: (l, 0, j)),
            pl.BlockSpec((None, 1, MOD_BLOCK_N), lambda l, j: (l, 0, j)),
        ],
        out_specs=pl.BlockSpec((None, MOD_ROWS, MOD_BLOCK_N), lambda l, j: (l, 0, j)),
        out_shape=jax.ShapeDtypeStruct((DEPTH, MOD_ROWS, n), F32),
        compiler_params=pltpu.CompilerParams(
            dimension_semantics=("arbitrary", "arbitrary"), vmem_limit_bytes=VMEM_LIMIT),
        name="modulation",
    )(cond, w_mod, b_mod.reshape(DEPTH, 1, n))


def _bias_kernel(rpb_ref, out_ref):
    cq = lax.broadcasted_iota(jnp.int32, (GRID_W, LANES), 0)
    ln = lax.broadcasted_iota(jnp.int32, (GRID_W, LANES), 1)
    ck = jnp.where(ln < GRID_W, ln, ln - GRID_W)
    cs = jnp.clip(cq - NA_KW // 2, 0, GRID_W - NA_KW)
    col_in = (ck >= cs) & (ck < cs + NA_KW)
    low_half = ln < GRID_W
    neg = jnp.full((GRID_W, LANES), NEG_INF, F32)
    lo = GRID_W - NA_KW

    for head in range(NA_HEADS):
        rp = rpb_ref[head]
        lane = lax.broadcasted_iota(jnp.int32, rp.shape, 1)
        rolled = pltpu.roll(rp, lo, axis=1)
        first = rp[:, 0:1]
        last = rp[:, 2 * NA_KW - 2:2 * NA_KW - 1]
        ext = jnp.where(lane < lo, first, jnp.where(lane > lo + 2 * NA_KW - 2, last, rolled))

        def toeplitz(tile, shift):
            dr = tile - 15
            if dr < -(NA_KH - 1) or dr > NA_KH - 1:
                return neg
            row = jnp.broadcast_to(ext[dr + NA_KH - 1:dr + NA_KH, :], (GRID_W, LANES))
            return pltpu.roll(row, shift, axis=1, stride=1, stride_axis=0)

        for copy in range(2):
            for col in range(BIAS_TILES // 2):
                t_lo = 2 * col + copy
                t_hi = t_lo + 1
                a = toeplitz(t_lo, GRID_W + 1)
                b = toeplitz(t_hi, 1)
                pair = jnp.where(low_half, a, b)
                pair = jnp.where(col_in, pair, NEG_INF)
                out_ref[head, copy, :, col * LANES:(col + 1) * LANES] = pair


def _bias_tables(rpb):
    rp = jnp.pad(rpb, ((0, 0), (0, 0), (0, 1), (0, LANES - (2 * NA_KW - 1))))
    width = BIAS_TILES * GRID_W
    return pl.pallas_call(
        _bias_kernel,
        grid=(DEPTH,),
        in_specs=[pl.BlockSpec((None, NA_HEADS, 2 * NA_KH, LANES), lambda l: (l, 0, 0, 0))],
        out_specs=pl.BlockSpec((None, NA_HEADS, 2, GRID_W, width), lambda l: (l, 0, 0, 0, 0)),
        out_shape=jax.ShapeDtypeStruct((DEPTH, NA_HEADS, 2, GRID_W, width), F32),
        compiler_params=pltpu.CompilerParams(
            dimension_semantics=("arbitrary",), vmem_limit_bytes=VMEM_LIMIT),
        name="bias_tables",
    )(rp)


def _in_proj(x_ref, mod_ref, row, g1_ref, win_ref, u_ref, n_tok):
    shift1 = mod_ref[pl.ds(row, 1), 0:D_MODEL]
    scale1 = mod_ref[pl.ds(row, 1), D_MODEL:2 * D_MODEL]
    for t0 in range(0, n_tok, TOK_TILE):
        x = x_ref[t0:t0 + TOK_TILE, :]
        h = ((_rms(x) * g1_ref[...]) * (1.0 + scale1) + shift1).astype(BF16)
        u_ref[t0:t0 + TOK_TILE, :] = _dot(h, win_ref[...])


def _softmax_pv(parts):
    m = None
    for s, _ in parts:
        mi = jnp.max(s, axis=-1, keepdims=True)
        m = mi if m is None else jnp.maximum(m, mi)
    den = None
    acc = None
    for s, v in parts:
        e = jnp.exp(s - m)
        li = jnp.sum(e, axis=-1, keepdims=True)
        oi = _dot(e.astype(BF16), v)
        den = li if den is None else den + li
        acc = oi if acc is None else acc + oi
    return acc / den


def _context_attention(u_ref, kv_ref, y_ref, n_tok):
    scale = NA_HEAD_DIM ** -0.5
    lane = lax.broadcasted_iota(jnp.int32, (n_tok, LANES), 1)
    low = lane < NA_HEAD_DIM
    for hp in range(NA_HEADS // 2):
        c0 = hp * LANES
        qpair = u_ref[:, c0:c0 + LANES] * scale
        kpair = kv_ref[:, c0:c0 + LANES]
        vpair = kv_ref[:, NA_WIDTH + c0:NA_WIDTH + c0 + LANES]
        outs = []
        for hh in range(2):
            q = jnp.where(low if hh == 0 else ~low, qpair, 0.0).astype(BF16)
            outs.append(_softmax_pv([(_dot_nt(q, kpair), vpair)]))
        y_ref[:, c0:c0 + LANES] = jnp.where(low, outs[0], outs[1]).astype(BF16)


def _window_start(r):
    return min(max(r - NA_KH // 2, 0), GRID_ROWS - NA_KH)


def _neighbourhood_attention(u_ref, kv_ref, ctx_ref, bias_ref, y_ref):
    scale = NA_HEAD_DIM ** -0.5
    blk_tok = ATT_BLOCK_ROWS * GRID_W
    key_tok = ATT_KEY_ROWS * GRID_W
    lane = lax.broadcasted_iota(jnp.int32, (blk_tok, LANES), 1)
    low = lane < NA_HEAD_DIM
    klane = lax.broadcasted_iota(jnp.int32, (GRID_W, key_tok), 1)
    for blk in range(GRID_ROWS // ATT_BLOCK_ROWS):
        r0 = blk * ATT_BLOCK_ROWS
        j0 = min(_window_start(r0), GRID_ROWS - ATT_KEY_ROWS)
        assert _window_start(r0 + ATT_BLOCK_ROWS - 1) + NA_KH <= j0 + ATT_KEY_ROWS
        q0 = r0 * GRID_W
        k0 = j0 * GRID_W
        for hp in range(NA_HEADS // 2):
            c0 = hp * LANES
            qpair = u_ref[q0:q0 + blk_tok, c0:c0 + LANES] * scale
            kpair = kv_ref[k0:k0 + key_tok, c0:c0 + LANES]
            vpair = kv_ref[k0:k0 + key_tok, NA_WIDTH + c0:NA_WIDTH + c0 + LANES]
            kctx = ctx_ref[:, c0:c0 + LANES]
            vctx = ctx_ref[:, NA_WIDTH + c0:NA_WIDTH + c0 + LANES]
            outs = []
            for hh in range(2):
                head = 2 * hp + hh
                q = jnp.where(low if hh == 0 else ~low, qpair, 0.0).astype(BF16)
                s_loc = _dot_nt(q, kpair)
                s_ctx = _dot_nt(q, kctx)
                strips = []
                for rr in range(ATT_BLOCK_ROWS):
                    r = r0 + rr
                    rs = _window_start(r)
                    tile0 = 15 + j0 - r
                    copy = tile0 % 2
                    b0 = (tile0 - copy) * GRID_W
                    bias = bias_ref[head, copy, :, b0:b0 + key_tok]
                    valid = (klane >= (rs - j0) * GRID_W) & (klane < (rs - j0 + NA_KH) * GRID_W)
                    s = s_loc[rr * GRID_W:(rr + 1) * GRID_W, :] + bias
                    strips.append(jnp.where(valid, s, NEG_INF))
                s_loc = jnp.concatenate(strips, axis=0)
                outs.append(_softmax_pv([(s_loc, vpair), (s_ctx, vctx)]))
            y_ref[q0:q0 + blk_tok, c0:c0 + LANES] = jnp.where(low, outs[0], outs[1]).astype(BF16)


def _pool_mixer(u_ref, pp_ref, wpool_ref, pscale_ref, y_ref, n_tok):
    c_in = 3 * NA_WIDTH
    zeros = jnp.zeros((POOL_PAD, POOL_WIDTH), F32)
    pp_ref[0:POOL_PAD, :] = zeros
    pp_ref[POOL_PAD + n_tok:2 * POOL_PAD + n_tok, :] = zeros
    pp_ref[POOL_PAD:POOL_PAD + n_tok, :] = u_ref[:, c_in:c_in + POOL_WIDTH]
    lane = lax.broadcasted_iota(jnp.int32, (MIX_CHUNK, LANES), 1)
    first = lane < POOL_GROUP_DIM
    tok = lax.broadcasted_iota(jnp.int32, (MIX_CHUNK, LANES), 0)
    wpool = wpool_ref[...].astype(BF16)

    for t0 in range(0, n_tok, MIX_CHUNK):
        def ld(d, half):
            a = POOL_PAD + t0 + d
            return pp_ref[a:a + MIX_CHUNK, half * LANES:(half + 1) * LANES]

        def window(half, lo, hi):
            acc = None
            for d in range(lo, hi):
                acc = ld(d, half) if acc is None else acc + ld(d, half)
            return acc

        a2 = window(0, -1, 1)
        a4 = a2 + ld(-2, 0) + ld(1, 0)
        a8 = window(1, -4, 4)
        a16 = a8 + window(1, -8, -4) + window(1, 4, 8)
        t = tok + t0
        halves = []
        for half, (small, big, ws, wb) in enumerate(((a2, a4, 1, 2), (a8, a16, 4, 8))):
            hw = jnp.where(first, ws, wb)
            cnt = (jnp.minimum(t + hw, n_tok) - jnp.maximum(t - hw, 0)).astype(F32)
            halves.append(jnp.where(first, small, big) / cnt - ld(0, half))
        pooled = jnp.concatenate(halves, axis=1).astype(BF16)
        y = _dot(pooled, wpool) * pscale_ref[...]
        y_ref[t0:t0 + MIX_CHUNK, NA_WIDTH:NA_WIDTH + POOL_WIDTH] = y.astype(BF16)


def _conv_mixer(u_ref, hp_ref, sh_ref, wdw_ref, bdw_ref, cng_ref, cnb_ref, y_ref, n_tok):
    c_a = 3 * NA_WIDTH + POOL_WIDTH
    c_g = c_a + CONV_WIDTH
    zeros = jnp.zeros((CONV_PAD, CONV_WIDTH), F32)
    hp_ref[0:CONV_PAD, :] = zeros
    hp_ref[CONV_PAD + n_tok:2 * CONV_PAD + n_tok, :] = zeros
    for t0 in range(0, n_tok, TOK_TILE):
        a = u_ref[t0:t0 + TOK_TILE, c_a:c_a + CONV_WIDTH]
        g = u_ref[t0:t0 + TOK_TILE, c_g:c_g + CONV_WIDTH]
        hp_ref[CONV_PAD + t0:CONV_PAD + t0 + TOK_TILE, :] = a * _sigmoid(g)
    off = CONV_PAD - CONV_K // 2
    span = MIX_CHUNK + (off + CONV_K - 1) // SUBLANES * SUBLANES
    for t0 in range(0, n_tok, MIX_CHUNK):
        for s in range(1, SUBLANES):
            sh_ref[s, 0:span, :] = hp_ref[t0 + s:t0 + s + span, :]
        acc = None
        for k in range(CONV_K):
            a, s = divmod(off + k, SUBLANES)
            if s == 0:
                win = hp_ref[t0 + a * SUBLANES:t0 + a * SUBLANES + MIX_CHUNK, :]
            else:
                win = sh_ref[s, a * SUBLANES:a * SUBLANES + MIX_CHUNK, :]
            term = wdw_ref[k:k + 1, :] * win
            acc = term if acc is None else acc + term
        acc = acc + bdw_ref[...]
        xc = acc - jnp.mean(acc, axis=-1, keepdims=True)
        y = xc * lax.rsqrt(jnp.mean(xc * xc, axis=-1, keepdims=True) + EPS)
        y = y * cng_ref[...] + cnb_ref[...]
        y_ref[t0:t0 + MIX_CHUNK, NA_WIDTH + POOL_WIDTH:D_MODEL] = (y * _sigmoid(y)).astype(BF16)


def _mix_prompt_kernel(layer_ref, x0_ref, xn_ref, mod_ref, g1_ref, win_ref, wpool_ref, pscale_ref,
                       wdw_ref, bdw_ref, cng_ref, cnb_ref,
                       cin_ref, coa_ref, cob_ref, coc_ref, cout_ref, cgu_ref, cdown_ref,
                       kbuf_ref, vbuf_ref, y_ref, k_ref, v_ref, wmix_ref,
                       bg_ref, boa_ref, bob_ref, boc_ref, bout_ref, bgu_ref, bdown_ref,
                       ua_ref, kv_ref, pp_ref, hp_ref, sh_ref, ub_ref):
    del layer_ref, kbuf_ref, vbuf_ref
    step = pl.program_id(0)

    @pl.when(step == 0)
    def _():
        wmix_ref[...] = win_ref[...].astype(BF16)
        _in_proj(x0_ref, mod_ref, 0, g1_ref, wmix_ref, ua_ref, SEQ)

    def mix(u_ref, u_next_ref):
        bg_ref[...] = cin_ref[:, N_MIX:].astype(BF16)
        for src, dst in ((coa_ref, boa_ref), (cob_ref, bob_ref), (coc_ref, boc_ref),
                         (cout_ref, bout_ref), (cgu_ref, bgu_ref), (cdown_ref, bdown_ref)):
            dst[...] = src[...].astype(BF16)
        _in_proj(xn_ref, mod_ref, 0, g1_ref, wmix_ref, u_next_ref, SEQ)
        k_ref[...] = u_ref[:, NA_WIDTH:2 * NA_WIDTH]
        v_ref[...] = u_ref[:, 2 * NA_WIDTH:3 * NA_WIDTH]
        kv_ref[...] = u_ref[:, NA_WIDTH:3 * NA_WIDTH].astype(BF16)
        _context_attention(u_ref, kv_ref, y_ref, SEQ)
        _pool_mixer(u_ref, pp_ref, wpool_ref, pscale_ref, y_ref, SEQ)
        _conv_mixer(u_ref, hp_ref, sh_ref, wdw_ref, bdw_ref, cng_ref, cnb_ref, y_ref, SEQ)

    @pl.when(step % 2 == 0)
    def _():
        mix(ua_ref, ub_ref)

    @pl.when(step % 2 == 1)
    def _():
        mix(ub_ref, ua_ref)


def _mix_sample_kernel(layer_ref, x_ref, mod_ref, g1_ref, wmix_ref, wpool_ref, pscale_ref, wdw_ref,
                       bdw_ref, cng_ref, cnb_ref, ck_ref, cv_ref, bias_ref, y_ref,
                       u_ref, kv_ref, ctx_ref, pp_ref, hp_ref, sh_ref):
    del layer_ref
    row = 1 + pl.program_id(0)
    _in_proj(x_ref, mod_ref, row, g1_ref, wmix_ref, u_ref, DEC_SEQ)
    kv_ref[...] = u_ref[:, NA_WIDTH:3 * NA_WIDTH].astype(BF16)
    ctx_ref[:, 0:NA_WIDTH] = ck_ref[...].astype(BF16)
    ctx_ref[:, NA_WIDTH:2 * NA_WIDTH] = cv_ref[...].astype(BF16)
    _neighbourhood_attention(u_ref, kv_ref, ctx_ref, bias_ref, y_ref)
    _pool_mixer(u_ref, pp_ref, wpool_ref, pscale_ref, y_ref, DEC_SEQ)
    _conv_mixer(u_ref, hp_ref, sh_ref, wdw_ref, bdw_ref, cng_ref, cnb_ref, y_ref, DEC_SEQ)


def _small_weight_specs():
    return [_layer_spec((POOL_WIDTH, POOL_WIDTH)), _layer_spec((1, POOL_WIDTH)),
            _layer_spec((CONV_K, CONV_WIDTH)), _layer_spec((1, CONV_WIDTH)),
            _layer_spec((1, CONV_WIDTH)), _layer_spec((1, CONV_WIDTH))]


def _mix_scratch(n_tok):
    return [
        pltpu.VMEM((n_tok, N_MIX), F32),
        pltpu.VMEM((n_tok, 2 * NA_WIDTH), BF16),
    ], [
        pltpu.VMEM((n_tok + 2 * POOL_PAD, POOL_WIDTH), F32),
        pltpu.VMEM((n_tok + 2 * CONV_PAD, CONV_WIDTH), F32),
        pltpu.VMEM((SUBLANES, MIX_CHUNK + 2 * CONV_PAD, CONV_WIDTH), F32),
    ]


def _mix_prompt(layer, x_all, mods, g1, w_in, small, cast_w, kbuf, vbuf):
    head, tail = _mix_scratch(SEQ)
    operands = (layer, x_all, x_all, mods, g1, w_in, *small, *cast_w, kbuf, vbuf)
    n_in = len(operands)
    chunk_specs, bf_specs, bf_shapes = [], [], []
    for rows, cols in CAST_WEIGHTS:
        r = rows // BATCH
        chunk_specs.append(pl.BlockSpec((None, r, cols), lambda b, l: (l[0], b, 0)))
        out_cols = N_GATE if cols == N_IN else cols
        bf_specs.append(pl.BlockSpec((r, out_cols), lambda b, l: (b, 0)))
        bf_shapes.append(jax.ShapeDtypeStruct((rows, out_cols), BF16))
    grid_spec = pltpu.PrefetchScalarGridSpec(
        num_scalar_prefetch=1,
        grid=(BATCH,),
        in_specs=[_whole_spec((SEQ, D_MODEL)),
                  pl.BlockSpec((SEQ, D_MODEL), lambda b, l: (jnp.minimum(b + 1, BATCH - 1), 0)),
                  _layer_spec((MOD_ROWS, N_MOD * D_MODEL)),
                  _layer_spec((1, D_MODEL)),
                  _layer_spec((D_MODEL, N_MIX))] + _small_weight_specs() + chunk_specs
        + [pl.BlockSpec(memory_space=pl.ANY), pl.BlockSpec(memory_space=pl.ANY)],
        out_specs=[pl.BlockSpec((SEQ, D_MODEL), lambda b, l: (b, 0)),
                   pl.BlockSpec((None, None, SEQ, NA_WIDTH), lambda b, l: (b, l[0], 0, 0)),
                   pl.BlockSpec((None, None, SEQ, NA_WIDTH), lambda b, l: (b, l[0], 0, 0)),
                   pl.BlockSpec((D_MODEL, N_MIX), lambda b, l: (0, 0))] + bf_specs,
        scratch_shapes=head + tail + [pltpu.VMEM((SEQ, N_MIX), F32)],
    )
    return pl.pallas_call(
        _mix_prompt_kernel,
        grid_spec=grid_spec,
        out_shape=[jax.ShapeDtypeStruct((N_PROMPT_TOK, D_MODEL), BF16),
                   jax.ShapeDtypeStruct(kbuf.shape, F32),
                   jax.ShapeDtypeStruct(vbuf.shape, F32),
                   jax.ShapeDtypeStruct((D_MODEL, N_MIX), BF16)] + bf_shapes,
        input_output_aliases={n_in - 2: 1, n_in - 1: 2},
        compiler_params=pltpu.CompilerParams(
            dimension_semantics=("arbitrary",), vmem_limit_bytes=VMEM_LIMIT),
        name="mix_prompt",
    )(*operands)


def _mix_sample(layer, x_all, mods, g1, wmix, small, cache_k, cache_v, bias):
    head, tail = _mix_scratch(DEC_SEQ)
    first_blk = N_PROMPT_TOK // DEC_SEQ
    ctx_spec = pl.BlockSpec((None, None, PAST_LEN, NA_WIDTH), lambda b, l: (b, l[0], 0, 0))
    grid_spec = pltpu.PrefetchScalarGridSpec(
        num_scalar_prefetch=1,
        grid=(DEC_BATCH,),
        in_specs=[pl.BlockSpec((DEC_SEQ, D_MODEL), lambda b, l: (first_blk + b, 0)),
                  _layer_spec((MOD_ROWS, N_MOD * D_MODEL)),
                  _layer_spec((1, D_MODEL)),
                  _whole_spec((D_MODEL, N_MIX))] + _small_weight_specs()
        + [ctx_spec, ctx_spec, _layer_spec((NA_HEADS, 2, GRID_W, BIAS_TILES * GRID_W))],
        out_specs=pl.BlockSpec((DEC_SEQ, D_MODEL), lambda b, l: (b, 0)),
        scratch_shapes=head + [pltpu.VMEM((PAST_LEN, 2 * NA_WIDTH), BF16)] + tail,
    )
    return pl.pallas_call(
        _mix_sample_kernel,
        grid_spec=grid_spec,
        out_shape=jax.ShapeDtypeStruct((N_SAMPLE_TOK, D_MODEL), BF16),
        compiler_params=pltpu.CompilerParams(
            dimension_semantics=("arbitrary",), vmem_limit_bytes=VMEM_LIMIT),
        name="mix_sample",
    )(layer, x_all, mods, g1, wmix, *small, cache_k, cache_v, bias)


FFN_TILE = 2 * TOK_TILE
PROMPT_TILES = N_PROMPT_TOK // FFN_TILE
SAMPLE_TILES = N_SAMPLE_TOK // FFN_TILE


def _ffn_kernel(final, layer_ref, x_ref, yp_ref, ys_ref, mod_ref, g1_ref, g2_ref, fg_ref, bg_ref,
                wg_ref, woa_ref, wob_ref, woc_ref, wout_ref, wgu_ref, wdown_ref, o_ref):
    del layer_ref
    i = pl.program_id(0)
    is_prompt = i < PROMPT_TILES
    row = jnp.where(is_prompt, 0, 1 + (i - PROMPT_TILES) // (DEC_SEQ // FFN_TILE))

    def mod(j):
        return mod_ref[pl.ds(row, 1), j * D_MODEL:(j + 1) * D_MODEL]

    branches = ((0, NA_WIDTH, woa_ref), (NA_WIDTH, NA_WIDTH + POOL_WIDTH, wob_ref),
                (NA_WIDTH + POOL_WIDTH, D_MODEL, woc_ref))
    for t0 in range(0, FFN_TILE, TOK_TILE):
        rows = slice(t0, t0 + TOK_TILE)
        x = x_ref[rows, :]
        h = ((_rms(x) * g1_ref[...]) * (1.0 + mod(1)) + mod(0)).astype(BF16)
        merged = None
        for br, (c0, c1, w_ref) in enumerate(branches):
            g0 = br * D_MODEL
            gate = _sigmoid(_dot(h, wg_ref[:, g0:g0 + D_MODEL]) + bg_ref[:, g0:g0 + D_MODEL])
            y = jnp.where(is_prompt, yp_ref[rows, c0:c1], ys_ref[rows, c0:c1])
            term = gate * _dot(y, w_ref[...])
            merged = term if merged is None else merged + term
        x1 = x + mod(2) * _dot(merged.astype(BF16), wout_ref[...])
        h2 = ((_rms(x1) * g2_ref[...]) * (1.0 + mod(4)) + mod(3)).astype(BF16)
        acc = None
        for c0, c1 in FFN_CHUNKS:
            a = _dot(h2, wgu_ref[:, c0:c1])
            b = _dot(h2, wgu_ref[:, FFN_HIDDEN + c0:FFN_HIDDEN + c1])
            t = ((a * _sigmoid(a)) * b).astype(BF16)
            d = _dot(t, wdown_ref[c0:c1, :])
            acc = d if acc is None else acc + d
        x2 = x1 + mod(5) * acc
        if final:
            x2 = _rms(x2) * fg_ref[...]
        o_ref[rows, :] = x2


def _ffn(layer, x_all, y_p, y_s, mods, g1, g2, fg, b_gate, weights, final):
    tile = pl.BlockSpec((FFN_TILE, D_MODEL), lambda i, l: (i, 0))
    yp_spec = pl.BlockSpec((FFN_TILE, D_MODEL), lambda i, l: (jnp.minimum(i, PROMPT_TILES - 1), 0))
    ys_spec = pl.BlockSpec((FFN_TILE, D_MODEL), lambda i, l: (jnp.maximum(i - PROMPT_TILES, 0), 0))
    grid_spec = pltpu.PrefetchScalarGridSpec(
        num_scalar_prefetch=1,
        grid=(N_TOK // FFN_TILE,),
        in_specs=[tile, yp_spec, ys_spec,
                  _layer_spec((MOD_ROWS, N_MOD * D_MODEL)),
                  _layer_spec((1, D_MODEL)), _layer_spec((1, D_MODEL)), _whole_spec((1, D_MODEL)),
                  _layer_spec((1, N_GATE)),
                  _whole_spec((D_MODEL, N_GATE)),
                  _whole_spec((NA_WIDTH, D_MODEL)), _whole_spec((POOL_WIDTH, D_MODEL)),
                  _whole_spec((CONV_WIDTH, D_MODEL)), _whole_spec((D_MODEL, D_MODEL)),
                  _whole_spec((D_MODEL, 2 * FFN_HIDDEN)), _whole_spec((FFN_HIDDEN, D_MODEL))],
        out_specs=tile,
    )
    return pl.pallas_call(
        functools.partial(_ffn_kernel, final),
        grid_spec=grid_spec,
        out_shape=jax.ShapeDtypeStruct((N_TOK, D_MODEL), F32),
        compiler_params=pltpu.CompilerParams(
            dimension_semantics=("arbitrary",), vmem_limit_bytes=VMEM_LIMIT),
        name="merge_ffn",
    )(layer, x_all, y_p, y_s, mods, g1, g2, fg, b_gate, *weights)


def _block_diag(w_pool):
    out = jnp.zeros((DEPTH, POOL_WIDTH, POOL_WIDTH), w_pool.dtype)
    for g in range(POOL_GROUPS):
        a = g * POOL_GROUP_DIM
        out = out.at[:, a:a + POOL_GROUP_DIM, a:a + POOL_GROUP_DIM].set(w_pool[:, g])
    return out


def kernel(x_prompt, x_sample, cache_k, cache_v, c, c_ctx, w_mod, b_mod, norm1_g, norm2_g, w_in,
           b_gate, rpb, w_oa, w_pool, pool_scale, w_ob, w_dw, b_dw, conv_norm_g, conv_norm_b,
           w_oc, w_out, w_gu, w_down, final_g):
    x_all = jnp.concatenate([x_prompt.reshape(N_PROMPT_TOK, D_MODEL),
                             x_sample.reshape(N_SAMPLE_TOK, D_MODEL)], axis=0)
    cond = jnp.concatenate([c_ctx[None, :], c,
                            jnp.zeros((MOD_ROWS - 1 - DEC_BATCH, D_MODEL), F32)], axis=0)
    mods = _modulation(cond, w_mod, b_mod)
    bias = _bias_tables(rpb)
    ck = cache_k.reshape(DEC_BATCH, DEPTH, PAST_LEN, NA_WIDTH)
    cv = cache_v.reshape(DEC_BATCH, DEPTH, PAST_LEN, NA_WIDTH)
    g1 = norm1_g.reshape(DEPTH, 1, D_MODEL)
    g2 = norm2_g.reshape(DEPTH, 1, D_MODEL)
    fg = final_g.reshape(1, D_MODEL)
    bg = b_gate.reshape(DEPTH, 1, N_GATE)
    small = (_block_diag(w_pool), pool_scale.reshape(DEPTH, 1, POOL_WIDTH), w_dw,
             b_dw.reshape(DEPTH, 1, CONV_WIDTH), conv_norm_g.reshape(DEPTH, 1, CONV_WIDTH),
             conv_norm_b.reshape(DEPTH, 1, CONV_WIDTH))
    cast_w = (w_in, w_oa, w_ob, w_oc, w_out, w_gu, w_down)
    kbuf = jnp.zeros((BATCH, DEPTH, SEQ, NA_WIDTH), F32)
    vbuf = jnp.zeros((BATCH, DEPTH, SEQ, NA_WIDTH), F32)
    for l in range(DEPTH):
        layer = jnp.full((1,), l, jnp.int32)
        y_p, kbuf, vbuf, wmix, *ffn_w = _mix_prompt(layer, x_all, mods, g1, w_in, small, cast_w,
                                                    kbuf, vbuf)
        y_s = _mix_sample(layer, x_all, mods, g1, wmix, small, ck, cv, bias)
        x_all = _ffn(layer, x_all, y_p, y_s, mods, g1, g2, fg, bg, ffn_w, l == DEPTH - 1)
    y_prompt = x_all[:N_PROMPT_TOK].reshape(BATCH, SEQ, D_MODEL)
    y_sample = x_all[N_PROMPT_TOK:].reshape(DEC_BATCH, DEC_SEQ, D_MODEL)
    kv_shape = (BATCH, DEPTH, SEQ, NA_HEADS, NA_HEAD_DIM)
    new_k = kbuf.reshape(kv_shape)
    new_v = vbuf.reshape(kv_shape)
    return (y_prompt, y_sample, new_k, new_v)
```

```python
import functools

import jax
import jax.numpy as jnp
from jax import lax
from jax.experimental import pallas as pl
from jax.experimental.pallas import tpu as pltpu

D_MODEL = 1024
BATCH = 16
SEQ = 256
DEPTH = 4
DEC_BATCH = 2
DEC_SEQ = 1024
PAST_LEN = 256
GRID_W = 64
GRID_ROWS = DEC_SEQ // GRID_W
NA_HEADS = 8
NA_HEAD_DIM = 64
NA_WIDTH = NA_HEADS * NA_HEAD_DIM
NA_KH = 8
NA_KW = 16
POOL_WIDTH = 256
POOL_GROUPS = 4
POOL_GROUP_DIM = 64
CONV_WIDTH = 256
CONV_K = 31
N_BRANCH = 3
N_MIX = 3 * NA_WIDTH + POOL_WIDTH + 2 * CONV_WIDTH
N_GATE = N_BRANCH * D_MODEL
N_IN = N_MIX + N_GATE
FFN_HIDDEN = 2816
N_MOD = 6
EPS = 1e-6
NEG_INF = -1e30

N_PROMPT_TOK = BATCH * SEQ
N_SAMPLE_TOK = DEC_BATCH * DEC_SEQ
N_TOK = N_PROMPT_TOK + N_SAMPLE_TOK

LANES = 128
SUBLANES = 8
MOD_ROWS = 8
MOD_BLOCK_N = 1536
TOK_TILE = 256
ATT_BLOCK_ROWS = 4
BIAS_TILES = 32
POOL_PAD = 8
CONV_PAD = 16
MIX_CHUNK = 128
PROJ_PIECE = 256
FILLER_TAPS = 6
FFN_CHUNKS = ((0, 1024), (1024, 2048), (2048, 2816))
VMEM_LIMIT = 56 * 1024 * 1024

F32 = jnp.float32
BF16 = jnp.bfloat16

CAST_WEIGHTS = ((D_MODEL, N_IN), (NA_WIDTH, D_MODEL), (POOL_WIDTH, D_MODEL), (CONV_WIDTH, D_MODEL),
                (D_MODEL, D_MODEL), (D_MODEL, 2 * FFN_HIDDEN), (FFN_HIDDEN, D_MODEL))


def _sigmoid(x):
    return 1.0 / (1.0 + jnp.exp(-x))


def _rms(x):
    return x * lax.rsqrt(jnp.mean(x * x, axis=-1, keepdims=True) + EPS)


def _dot(a, b):
    return jnp.dot(a, b, preferred_element_type=F32)


def _dot_nt(a, b):
    return lax.dot_general(a, b, (((1,), (1,)), ((), ())), preferred_element_type=F32)


def _layer_spec(shape):
    zeros = (0,) * len(shape)
    return pl.BlockSpec((None,) + tuple(shape), lambda i, l: (l[0],) + zeros,
                        pipeline_mode=pl.Buffered(1))


def _whole_spec(shape):
    zeros = (0,) * len(shape)
    return pl.BlockSpec(tuple(shape), lambda i, l: zeros, pipeline_mode=pl.Buffered(1))


def _mod_kernel(cond_ref, w_ref, b_ref, out_ref):
    c = cond_ref[...]
    s = (c * _sigmoid(c)).astype(BF16)
    out_ref[...] = _dot(s, w_ref[...].astype(BF16)) + b_ref[...]


def _modulation(cond, w_mod, b_mod):
    n = N_MOD * D_MODEL
    return pl.pallas_call(
        _mod_kernel,
        grid=(DEPTH, n // MOD_BLOCK_N),
        in_specs=[
            pl.BlockSpec((MOD_ROWS, D_MODEL), lambda l, j: (0, 0)),
            pl.BlockSpec((None, D_MODEL, MOD_BLOCK_N), lambda l, j: (l, 0, j)),
            pl.BlockSpec((None, 1, MOD_BLOCK_N), lambda l, j: (l, 0, j)),
        ],
        out_specs=pl.BlockSpec((None, MOD_ROWS, MOD_BLOCK_N), lambda l, j: (l, 0, j)),
        out_shape=jax.ShapeDtypeStruct((DEPTH, MOD_ROWS, n), F32),
        compiler_params=pltpu.CompilerParams(
            dimension_semantics=("arbitrary", "arbitrary"), vmem_limit_bytes=VMEM_LIMIT),
        name="modulation",
    )(cond, w_mod, b_mod.reshape(DEPTH, 1, n))


def _bias_kernel(rpb_ref, out_ref):
    cq = lax.broadcasted_iota(jnp.int32, (GRID_W, LANES), 0)
    ln = lax.broadcasted_iota(jnp.int32, (GRID_W, LANES), 1)
    ck = jnp.where(ln < GRID_W, ln, ln - GRID_W)
    cs = jnp.clip(cq - NA_KW // 2, 0, GRID_W - NA_KW)
    col_in = (ck >= cs) & (ck < cs + NA_KW)
    low_half = ln < GRID_W
    neg = jnp.full((GRID_W, LANES), NEG_INF, F32)
    lo = GRID_W - NA_KW

    for head in range(NA_HEADS):
        rp = rpb_ref[head]
        lane = lax.broadcasted_iota(jnp.int32, rp.shape, 1)
        rolled = pltpu.roll(rp, lo, axis=1)
        first = rp[:, 0:1]
        last = rp[:, 2 * NA_KW - 2:2 * NA_KW - 1]
        ext = jnp.where(lane < lo, first, jnp.where(lane > lo + 2 * NA_KW - 2, last, rolled))

        def toeplitz(tile, shift):
            dr = tile - 15
            if dr < -(NA_KH - 1) or dr > NA_KH - 1:
                return neg
            row = jnp.broadcast_to(ext[dr + NA_KH - 1:dr + NA_KH, :], (GRID_W, LANES))
            return pltpu.roll(row, shift, axis=1, stride=1, stride_axis=0)

        for copy in range(2):
            for col in range(BIAS_TILES // 2):
                t_lo = 2 * col + copy
                t_hi = t_lo + 1
                a = toeplitz(t_lo, GRID_W + 1)
                b = toeplitz(t_hi, 1)
                pair = jnp.where(low_half, a, b)
                pair = jnp.where(col_in, pair, NEG_INF)
                out_ref[head, copy, :, col * LANES:(col + 1) * LANES] = pair


def _bias_tables(rpb):
    rp = jnp.pad(rpb, ((0, 0), (0, 0), (0, 1), (0, LANES - (2 * NA_KW - 1))))
    width = BIAS_TILES * GRID_W
    return pl.pallas_call(
        _bias_kernel,
        grid=(DEPTH,),
        in_specs=[pl.BlockSpec((None, NA_HEADS, 2 * NA_KH, LANES), lambda l: (l, 0, 0, 0))],
        out_specs=pl.BlockSpec((None, NA_HEADS, 2, GRID_W, width), lambda l: (l, 0, 0, 0, 0)),
        out_shape=jax.ShapeDtypeStruct((DEPTH, NA_HEADS, 2, GRID_W, width), F32),
        compiler_params=pltpu.CompilerParams(
            dimension_semantics=("arbitrary",), vmem_limit_bytes=VMEM_LIMIT),
        name="bias_tables",
    )(rp)


def _in_proj(x_ref, mod_ref, row, g1_ref, win_ref, u_ref, n_tok):
    shift1 = mod_ref[pl.ds(row, 1), 0:D_MODEL]
    scale1 = mod_ref[pl.ds(row, 1), D_MODEL:2 * D_MODEL]
    for t0 in range(0, n_tok, TOK_TILE):
        x = x_ref[t0:t0 + TOK_TILE, :]
        h = ((_rms(x) * g1_ref[...]) * (1.0 + scale1) + shift1).astype(BF16)
        u_ref[t0:t0 + TOK_TILE, :] = _dot(h, win_ref[...])


def _softmax_pv(parts):
    m = None
    for s, _ in parts:
        mi = jnp.max(s, axis=-1, keepdims=True)
        m = mi if m is None else jnp.maximum(m, mi)
    den = None
    acc = None
    for s, v in parts:
        e = jnp.exp(s - m)
        li = jnp.sum(e, axis=-1, keepdims=True)
        oi = _dot(e.astype(BF16), v)
        den = li if den is None else den + li
        acc = oi if acc is None else acc + oi
    return acc / den


def _context_attention(u_ref, kv_ref, y_ref, n_tok):
    scale = NA_HEAD_DIM ** -0.5
    lane = lax.broadcasted_iota(jnp.int32, (n_tok, LANES), 1)
    low = lane < NA_HEAD_DIM
    for hp in range(NA_HEADS // 2):
        c0 = hp * LANES
        qpair = u_ref[:, c0:c0 + LANES] * scale
        kpair = kv_ref[:, c0:c0 + LANES]
        vpair = kv_ref[:, NA_WIDTH + c0:NA_WIDTH + c0 + LANES]
        outs = []
        for hh in range(2):
            q = jnp.where(low if hh == 0 else ~low, qpair, 0.0).astype(BF16)
            outs.append(_softmax_pv([(_dot_nt(q, kpair), vpair)]))
        y_ref[:, c0:c0 + LANES] = jnp.where(low, outs[0], outs[1]).astype(BF16)


def _window_start(r):
    return min(max(r - NA_KH // 2, 0), GRID_ROWS - NA_KH)


def _neighbourhood_attention(u_ref, kv_ref, ctx_ref, bias_ref, y_ref):
    scale = NA_HEAD_DIM ** -0.5
    blk_tok = ATT_BLOCK_ROWS * GRID_W
    lane = lax.broadcasted_iota(jnp.int32, (blk_tok, LANES), 1)
    low = lane < NA_HEAD_DIM
    for blk in range(GRID_ROWS // ATT_BLOCK_ROWS):
        r0 = blk * ATT_BLOCK_ROWS
        first_row = _window_start(r0)
        key_rows = _window_start(r0 + ATT_BLOCK_ROWS - 1) + NA_KH - first_row
        key_rows += key_rows % 2
        j0 = min(first_row, GRID_ROWS - key_rows)
        key_tok = key_rows * GRID_W
        klane = lax.broadcasted_iota(jnp.int32, (GRID_W, key_tok), 1)
        q0 = r0 * GRID_W
        k0 = j0 * GRID_W
        for hp in range(NA_HEADS // 2):
            c0 = hp * LANES
            qpair = u_ref[q0:q0 + blk_tok, c0:c0 + LANES] * scale
            kpair = kv_ref[k0:k0 + key_tok, c0:c0 + LANES]
            vpair = kv_ref[k0:k0 + key_tok, NA_WIDTH + c0:NA_WIDTH + c0 + LANES]
            kctx = ctx_ref[:, c0:c0 + LANES]
            vctx = ctx_ref[:, NA_WIDTH + c0:NA_WIDTH + c0 + LANES]
            outs = []
            for hh in range(2):
                head = 2 * hp + hh
                q = jnp.where(low if hh == 0 else ~low, qpair, 0.0).astype(BF16)
                s_loc = _dot_nt(q, kpair)
                s_ctx = _dot_nt(q, kctx)
                strips = []
                for rr in range(ATT_BLOCK_ROWS):
                    r = r0 + rr
                    rs = _window_start(r)
                    tile0 = 15 + j0 - r
                    copy = tile0 % 2
                    b0 = (tile0 - copy) * GRID_W
                    bias = bias_ref[head, copy, :, b0:b0 + key_tok]
                    s = s_loc[rr * GRID_W:(rr + 1) * GRID_W, :] + bias
                    if key_rows > NA_KH:
                        valid = ((klane >= (rs - j0) * GRID_W)
                                 & (klane < (rs - j0 + NA_KH) * GRID_W))
                        s = jnp.where(valid, s, NEG_INF)
                    else:
                        assert rs == j0
                    strips.append(s)
                s_loc = jnp.concatenate(strips, axis=0)
                outs.append(_softmax_pv([(s_loc, vpair), (s_ctx, vctx)]))
            y_ref[q0:q0 + blk_tok, c0:c0 + LANES] = jnp.where(low, outs[0], outs[1]).astype(BF16)


def _run_filler(fillers):
    if fillers:
        fillers.pop(0)()


def _pool_mixer(u_ref, pp_ref, wpool_ref, pscale_ref, y_ref, n_tok, fillers=()):
    c_in = 3 * NA_WIDTH
    zeros = jnp.zeros((POOL_PAD, POOL_WIDTH), F32)
    pp_ref[0:POOL_PAD, :] = zeros
    pp_ref[POOL_PAD + n_tok:2 * POOL_PAD + n_tok, :] = zeros
    pp_ref[POOL_PAD:POOL_PAD + n_tok, :] = u_ref[:, c_in:c_in + POOL_WIDTH]
    lane = lax.broadcasted_iota(jnp.int32, (MIX_CHUNK, LANES), 1)
    first = lane < POOL_GROUP_DIM
    tok = lax.broadcasted_iota(jnp.int32, (MIX_CHUNK, LANES), 0)
    wpool = wpool_ref[...].astype(BF16)

    for t0 in range(0, n_tok, MIX_CHUNK):
        def ld(d, half):
            a = POOL_PAD + t0 + d
            return pp_ref[a:a + MIX_CHUNK, half * LANES:(half + 1) * LANES]

        def window(half, lo, hi):
            acc = None
            for d in range(lo, hi):
                acc = ld(d, half) if acc is None else acc + ld(d, half)
            return acc

        a2 = window(0, -1, 1)
        a4 = a2 + ld(-2, 0) + ld(1, 0)
        a8 = window(1, -4, 4)
        a16 = a8 + window(1, -8, -4) + window(1, 4, 8)
        t = tok + t0
        halves = []
        for half, (small, big, ws, wb) in enumerate(((a2, a4, 1, 2), (a8, a16, 4, 8))):
            hw = jnp.where(first, ws, wb)
            cnt = (jnp.minimum(t + hw, n_tok) - jnp.maximum(t - hw, 0)).astype(F32)
            halves.append(jnp.where(first, small, big) / cnt - ld(0, half))
        pooled = jnp.concatenate(halves, axis=1).astype(BF16)
        y = _dot(pooled, wpool) * pscale_ref[...]
        y_ref[t0:t0 + MIX_CHUNK, NA_WIDTH:NA_WIDTH + POOL_WIDTH] = y.astype(BF16)
        _run_filler(fillers)


def _conv_mixer(u_ref, hp_ref, sh_ref, wdw_ref, bdw_ref, cng_ref, cnb_ref, y_ref, n_tok,
                fillers=()):
    c_a = 3 * NA_WIDTH + POOL_WIDTH
    c_g = c_a + CONV_WIDTH
    zeros = jnp.zeros((CONV_PAD, CONV_WIDTH), F32)
    hp_ref[0:CONV_PAD, :] = zeros
    hp_ref[CONV_PAD + n_tok:2 * CONV_PAD + n_tok, :] = zeros
    for t0 in range(0, n_tok, TOK_TILE):
        a = u_ref[t0:t0 + TOK_TILE, c_a:c_a + CONV_WIDTH]
        g = u_ref[t0:t0 + TOK_TILE, c_g:c_g + CONV_WIDTH]
        hp_ref[CONV_PAD + t0:CONV_PAD + t0 + TOK_TILE, :] = a * _sigmoid(g)
    off = CONV_PAD - CONV_K // 2
    span = MIX_CHUNK + (off + CONV_K - 1) // SUBLANES * SUBLANES
    for t0 in range(0, n_tok, MIX_CHUNK):
        for s in range(1, SUBLANES):
            sh_ref[s, 0:span, :] = hp_ref[t0 + s:t0 + s + span, :]
        acc = None
        for k in range(CONV_K):
            a, s = divmod(off + k, SUBLANES)
            if s == 0:
                win = hp_ref[t0 + a * SUBLANES:t0 + a * SUBLANES + MIX_CHUNK, :]
            else:
                win = sh_ref[s, a * SUBLANES:a * SUBLANES + MIX_CHUNK, :]
            term = wdw_ref[k:k + 1, :] * win
            acc = term if acc is None else acc + term
            if k % FILLER_TAPS == FILLER_TAPS - 1:
                _run_filler(fillers)
        acc = acc + bdw_ref[...]
        xc = acc - jnp.mean(acc, axis=-1, keepdims=True)
        y = xc * lax.rsqrt(jnp.mean(xc * xc, axis=-1, keepdims=True) + EPS)
        y = y * cng_ref[...] + cnb_ref[...]
        y_ref[t0:t0 + MIX_CHUNK, NA_WIDTH + POOL_WIDTH:D_MODEL] = (y * _sigmoid(y)).astype(BF16)


def _mix_prompt_kernel(layer_ref, x0_ref, xn_ref, mod_ref, g1_ref, win_ref, wpool_ref, pscale_ref,
                       wdw_ref, bdw_ref, cng_ref, cnb_ref,
                       cin_ref, coa_ref, cob_ref, coc_ref, cout_ref, cgu_ref, cdown_ref,
                       kbuf_ref, vbuf_ref, y_ref, k_ref, v_ref, wmix_ref,
                       bg_ref, boa_ref, bob_ref, boc_ref, bout_ref, bgu_ref, bdown_ref,
                       ua_ref, kv_ref, pp_ref, hp_ref, sh_ref, ub_ref):
    del layer_ref, kbuf_ref, vbuf_ref
    step = pl.program_id(0)

    @pl.when(step == 0)
    def _():
        wmix_ref[...] = win_ref[...].astype(BF16)
        _in_proj(x0_ref, mod_ref, 0, g1_ref, wmix_ref, ua_ref, SEQ)

    def mix(u_ref, u_next_ref):
        bg_ref[...] = cin_ref[:, N_MIX:].astype(BF16)
        for src, dst in ((coa_ref, boa_ref), (cob_ref, bob_ref), (coc_ref, boc_ref),
                         (cout_ref, bout_ref), (cgu_ref, bgu_ref), (cdown_ref, bdown_ref)):
            dst[...] = src[...].astype(BF16)
        shift1 = mod_ref[0:1, 0:D_MODEL]
        scale1 = mod_ref[0:1, D_MODEL:2 * D_MODEL]
        h_next = ((_rms(xn_ref[...]) * g1_ref[...]) * (1.0 + scale1) + shift1).astype(BF16)

        def piece(c0):
            def run():
                u_next_ref[:, c0:c0 + PROJ_PIECE] = _dot(h_next, wmix_ref[:, c0:c0 + PROJ_PIECE])
            return run

        fillers = [piece(c0) for c0 in range(0, N_MIX, PROJ_PIECE)]
        k_ref[...] = u_ref[:, NA_WIDTH:2 * NA_WIDTH]
        v_ref[...] = u_ref[:, 2 * NA_WIDTH:3 * NA_WIDTH]
        kv_ref[...] = u_ref[:, NA_WIDTH:3 * NA_WIDTH].astype(BF16)
        _context_attention(u_ref, kv_ref, y_ref, SEQ)
        _pool_mixer(u_ref, pp_ref, wpool_ref, pscale_ref, y_ref, SEQ, fillers)
        _conv_mixer(u_ref, hp_ref, sh_ref, wdw_ref, bdw_ref, cng_ref, cnb_ref, y_ref, SEQ, fillers)
        while fillers:
            _run_filler(fillers)

    @pl.when(step % 2 == 0)
    def _():
        mix(ua_ref, ub_ref)

    @pl.when(step % 2 == 1)
    def _():
        mix(ub_ref, ua_ref)


def _mix_sample_kernel(layer_ref, x_ref, mod_ref, g1_ref, wmix_ref, wpool_ref, pscale_ref, wdw_ref,
                       bdw_ref, cng_ref, cnb_ref, ck_ref, cv_ref, bias_ref, y_ref,
                       u_ref, kv_ref, ctx_ref, pp_ref, hp_ref, sh_ref):
    del layer_ref
    row = 1 + pl.program_id(0)
    _in_proj(x_ref, mod_ref, row, g1_ref, wmix_ref, u_ref, DEC_SEQ)
    kv_ref[...] = u_ref[:, NA_WIDTH:3 * NA_WIDTH].astype(BF16)
    ctx_ref[:, 0:NA_WIDTH] = ck_ref[...].astype(BF16)
    ctx_ref[:, NA_WIDTH:2 * NA_WIDTH] = cv_ref[...].astype(BF16)
    _neighbourhood_attention(u_ref, kv_ref, ctx_ref, bias_ref, y_ref)
    _pool_mixer(u_ref, pp_ref, wpool_ref, pscale_ref, y_ref, DEC_SEQ)
    _conv_mixer(u_ref, hp_ref, sh_ref, wdw_ref, bdw_ref, cng_ref, cnb_ref, y_ref, DEC_SEQ)


def _small_weight_specs():
    return [_layer_spec((POOL_WIDTH, POOL_WIDTH)), _layer_spec((1, POOL_WIDTH)),
            _layer_spec((CONV_K, CONV_WIDTH)), _layer_spec((1, CONV_WIDTH)),
            _layer_spec((1, CONV_WIDTH)), _layer_spec((1, CONV_WIDTH))]


def _mix_scratch(n_tok):
    return [
        pltpu.VMEM((n_tok, N_MIX), F32),
        pltpu.VMEM((n_tok, 2 * NA_WIDTH), BF16),
    ], [
        pltpu.VMEM((n_tok + 2 * POOL_PAD, POOL_WIDTH), F32),
        pltpu.VMEM((n_tok + 2 * CONV_PAD, CONV_WIDTH), F32),
        pltpu.VMEM((SUBLANES, MIX_CHUNK + 2 * CONV_PAD, CONV_WIDTH), F32),
    ]


def _mix_prompt(layer, x_all, mods, g1, w_in, small, cast_w, kbuf, vbuf):
    head, tail = _mix_scratch(SEQ)
    operands = (layer, x_all, x_all, mods, g1, w_in, *small, *cast_w, kbuf, vbuf)
    n_in = len(operands)
    chunk_specs, bf_specs, bf_shapes = [], [], []
    for rows, cols in CAST_WEIGHTS:
        r = rows // BATCH
        chunk_specs.append(pl.BlockSpec((None, r, cols), lambda b, l: (l[0], b, 0)))
        out_cols = N_GATE if cols == N_IN else cols
        bf_specs.append(pl.BlockSpec((r, out_cols), lambda b, l: (b, 0)))
        bf_shapes.append(jax.ShapeDtypeStruct((rows, out_cols), BF16))
    grid_spec = pltpu.PrefetchScalarGridSpec(
        num_scalar_prefetch=1,
        grid=(BATCH,),
        in_specs=[_whole_spec((SEQ, D_MODEL)),
                  pl.BlockSpec((SEQ, D_MODEL), lambda b, l: (jnp.minimum(b + 1, BATCH - 1), 0)),
                  _layer_spec((MOD_ROWS, N_MOD * D_MODEL)),
                  _layer_spec((1, D_MODEL)),
                  _layer_spec((D_MODEL, N_MIX))] + _small_weight_specs() + chunk_specs
        + [pl.BlockSpec(memory_space=pl.ANY), pl.BlockSpec(memory_space=pl.ANY)],
        out_specs=[pl.BlockSpec((SEQ, D_MODEL), lambda b, l: (b, 0)),
                   pl.BlockSpec((None, None, SEQ, NA_WIDTH), lambda b, l: (b, l[0], 0, 0)),
                   pl.BlockSpec((None, None, SEQ, NA_WIDTH), lambda b, l: (b, l[0], 0, 0)),
                   pl.BlockSpec((D_MODEL, N_MIX), lambda b, l: (0, 0))] + bf_specs,
        scratch_shapes=head + tail + [pltpu.VMEM((SEQ, N_MIX), F32)],
    )
    return pl.pallas_call(
        _mix_prompt_kernel,
        grid_spec=grid_spec,
        out_shape=[jax.ShapeDtypeStruct((N_PROMPT_TOK, D_MODEL), BF16),
                   jax.ShapeDtypeStruct(kbuf.shape, F32),
                   jax.ShapeDtypeStruct(vbuf.shape, F32),
                   jax.ShapeDtypeStruct((D_MODEL, N_MIX), BF16)] + bf_shapes,
        input_output_aliases={n_in - 2: 1, n_in - 1: 2},
        compiler_params=pltpu.CompilerParams(
            dimension_semantics=("arbitrary",), vmem_limit_bytes=VMEM_LIMIT),
        name="mix_prompt",
    )(*operands)


def _mix_sample(layer, x_all, mods, g1, wmix, small, cache_k, cache_v, bias):
    head, tail = _mix_scratch(DEC_SEQ)
    first_blk = N_PROMPT_TOK // DEC_SEQ
    ctx_spec = pl.BlockSpec((None, None, PAST_LEN, NA_WIDTH), lambda b, l: (b, l[0], 0, 0))
    grid_spec = pltpu.PrefetchScalarGridSpec(
        num_scalar_prefetch=1,
        grid=(DEC_BATCH,),
        in_specs=[pl.BlockSpec((DEC_SEQ, D_MODEL), lambda b, l: (first_blk + b, 0)),
                  _layer_spec((MOD_ROWS, N_MOD * D_MODEL)),
                  _layer_spec((1, D_MODEL)),
                  _whole_spec((D_MODEL, N_MIX))] + _small_weight_specs()
        + [ctx_spec, ctx_spec, _layer_spec((NA_HEADS, 2, GRID_W, BIAS_TILES * GRID_W))],
        out_specs=pl.BlockSpec((DEC_SEQ, D_MODEL), lambda b, l: (b, 0)),
        scratch_shapes=head + [pltpu.VMEM((PAST_LEN, 2 * NA_WIDTH), BF16)] + tail,
    )
    return pl.pallas_call(
        _mix_sample_kernel,
        grid_spec=grid_spec,
        out_shape=jax.ShapeDtypeStruct((N_SAMPLE_TOK, D_MODEL), BF16),
        compiler_params=pltpu.CompilerParams(
            dimension_semantics=("arbitrary",), vmem_limit_bytes=VMEM_LIMIT),
        name="mix_sample",
    )(layer, x_all, mods, g1, wmix, *small, cache_k, cache_v, bias)


FFN_TILE = 2 * TOK_TILE
PROMPT_TILES = N_PROMPT_TOK // FFN_TILE
SAMPLE_TILES = N_SAMPLE_TOK // FFN_TILE


def _ffn_kernel(final, layer_ref, x_ref, yp_ref, ys_ref, mod_ref, g1_ref, g2_ref, fg_ref, bg_ref,
                wg_ref, woa_ref, wob_ref, woc_ref, wout_ref, wgu_ref, wdown_ref, o_ref):
    del layer_ref
    i = pl.program_id(0)
    is_prompt = i < PROMPT_TILES
    row = jnp.where(is_prompt, 0, 1 + (i - PROMPT_TILES) // (DEC_SEQ // FFN_TILE))

    def mod(j):
        return mod_ref[pl.ds(row, 1), j * D_MODEL:(j + 1) * D_MODEL]

    branches = ((0, NA_WIDTH, woa_ref), (NA_WIDTH, NA_WIDTH + POOL_WIDTH, wob_ref),
                (NA_WIDTH + POOL_WIDTH, D_MODEL, woc_ref))
    halves = [slice(t0, t0 + TOK_TILE) for t0 in range(0, FFN_TILE, TOK_TILE)]
    xs = [x_ref[rows, :] for rows in halves]
    hs = [((_rms(x) * g1_ref[...]) * (1.0 + mod(1)) + mod(0)).astype(BF16) for x in xs]
    merged = [None] * len(halves)
    for br, (c0, c1, w_ref) in enumerate(branches):
        g0 = br * D_MODEL
        gates = [_sigmoid(_dot(h, wg_ref[:, g0:g0 + D_MODEL]) + bg_ref[:, g0:g0 + D_MODEL])
                 for h in hs]
        ys = [jnp.where(is_prompt, yp_ref[rows, c0:c1], ys_ref[rows, c0:c1]) for rows in halves]
        terms = [gate * _dot(y, w_ref[...]) for gate, y in zip(gates, ys)]
        merged = [t if m is None else m + t for m, t in zip(merged, terms)]
    x1s = [x + mod(2) * _dot(m.astype(BF16), wout_ref[...]) for x, m in zip(xs, merged)]
    h2s = [((_rms(x1) * g2_ref[...]) * (1.0 + mod(4)) + mod(3)).astype(BF16) for x1 in x1s]
    accs = [None] * len(halves)
    for c0, c1 in FFN_CHUNKS:
        a_s = [_dot(h2, wgu_ref[:, c0:c1]) for h2 in h2s]
        b_s = [_dot(h2, wgu_ref[:, FFN_HIDDEN + c0:FFN_HIDDEN + c1]) for h2 in h2s]
        ts = [((a * _sigmoid(a)) * b).astype(BF16) for a, b in zip(a_s, b_s)]
        ds = [_dot(t, wdown_ref[c0:c1, :]) for t in ts]
        accs = [d if acc is None else acc + d for acc, d in zip(accs, ds)]
    for rows, x1, acc in zip(halves, x1s, accs):
        x2 = x1 + mod(5) * acc
        if final:
            x2 = _rms(x2) * fg_ref[...]
        o_ref[rows, :] = x2


def _ffn(layer, x_all, y_p, y_s, mods, g1, g2, fg, b_gate, weights, final):
    tile = pl.BlockSpec((FFN_TILE, D_MODEL), lambda i, l: (i, 0))
    yp_spec = pl.BlockSpec((FFN_TILE, D_MODEL), lambda i, l: (jnp.minimum(i, PROMPT_TILES - 1), 0))
    ys_spec = pl.BlockSpec((FFN_TILE, D_MODEL), lambda i, l: (jnp.maximum(i - PROMPT_TILES, 0), 0))
    grid_spec = pltpu.PrefetchScalarGridSpec(
        num_scalar_prefetch=1,
        grid=(N_TOK // FFN_TILE,),
        in_specs=[tile, yp_spec, ys_spec,
                  _layer_spec((MOD_ROWS, N_MOD * D_MODEL)),
                  _layer_spec((1, D_MODEL)), _layer_spec((1, D_MODEL)), _whole_spec((1, D_MODEL)),
                  _layer_spec((1, N_GATE)),
                  _whole_spec((D_MODEL, N_GATE)),
                  _whole_spec((NA_WIDTH, D_MODEL)), _whole_spec((POOL_WIDTH, D_MODEL)),
                  _whole_spec((CONV_WIDTH, D_MODEL)), _whole_spec((D_MODEL, D_MODEL)),
                  _whole_spec((D_MODEL, 2 * FFN_HIDDEN)), _whole_spec((FFN_HIDDEN, D_MODEL))],
        out_specs=tile,
    )
    return pl.pallas_call(
        functools.partial(_ffn_kernel, final),
        grid_spec=grid_spec,
        out_shape=jax.ShapeDtypeStruct((N_TOK, D_MODEL), F32),
        compiler_params=pltpu.CompilerParams(
            dimension_semantics=("arbitrary",), vmem_limit_bytes=VMEM_LIMIT),
        name="merge_ffn",
    )(layer, x_all, y_p, y_s, mods, g1, g2, fg, b_gate, *weights)


def _block_diag(w_pool):
    out = jnp.zeros((DEPTH, POOL_WIDTH, POOL_WIDTH), w_pool.dtype)
    for g in range(POOL_GROUPS):
        a = g * POOL_GROUP_DIM
        out = out.at[:, a:a + POOL_GROUP_DIM, a:a + POOL_GROUP_DIM].set(w_pool[:, g])
    return out


def kernel(x_prompt, x_sample, cache_k, cache_v, c, c_ctx, w_mod, b_mod, norm1_g, norm2_g, w_in,
           b_gate, rpb, w_oa, w_pool, pool_scale, w_ob, w_dw, b_dw, conv_norm_g, conv_norm_b,
           w_oc, w_out, w_gu, w_down, final_g):
    x_all = jnp.concatenate([x_prompt.reshape(N_PROMPT_TOK, D_MODEL),
                             x_sample.reshape(N_SAMPLE_TOK, D_MODEL)], axis=0)
    cond = jnp.concatenate([c_ctx[None, :], c,
                            jnp.zeros((MOD_ROWS - 1 - DEC_BATCH, D_MODEL), F32)], axis=0)
    mods = _modulation(cond, w_mod, b_mod)
    bias = _bias_tables(rpb)
    ck = cache_k.reshape(DEC_BATCH, DEPTH, PAST_LEN, NA_WIDTH)
    cv = cache_v.reshape(DEC_BATCH, DEPTH, PAST_LEN, NA_WIDTH)
    g1 = norm1_g.reshape(DEPTH, 1, D_MODEL)
    g2 = norm2_g.reshape(DEPTH, 1, D_MODEL)
    fg = final_g.reshape(1, D_MODEL)
    bg = b_gate.reshape(DEPTH, 1, N_GATE)
    small = (_block_diag(w_pool), pool_scale.reshape(DEPTH, 1, POOL_WIDTH), w_dw,
             b_dw.reshape(DEPTH, 1, CONV_WIDTH), conv_norm_g.reshape(DEPTH, 1, CONV_WIDTH),
             conv_norm_b.reshape(DEPTH, 1, CONV_WIDTH))
    cast_w = (w_in, w_oa, w_ob, w_oc, w_out, w_gu, w_down)
    kbuf = jnp.zeros((BATCH, DEPTH, SEQ, NA_WIDTH), F32)
    vbuf = jnp.zeros((BATCH, DEPTH, SEQ, NA_WIDTH), F32)
    for l in range(DEPTH):
        layer = jnp.full((1,), l, jnp.int32)
        y_p, kbuf, vbuf, wmix, *ffn_w = _mix_prompt(layer, x_all, mods, g1, w_in, small, cast_w,
                                                    kbuf, vbuf)
        y_s = _mix_sample(layer, x_all, mods, g1, wmix, small, ck, cv, bias)
        x_all = _ffn(layer, x_all, y_p, y_s, mods, g1, g2, fg, bg, ffn_w, l == DEPTH - 1)
    y_prompt = x_all[:N_PROMPT_TOK].reshape(BATCH, SEQ, D_MODEL)
    y_sample = x_all[N_PROMPT_TOK:].reshape(DEC_BATCH, DEC_SEQ, D_MODEL)
    kv_shape = (BATCH, DEPTH, SEQ, NA_HEADS, NA_HEAD_DIM)
    new_k = kbuf.reshape(kv_shape)
    new_v = vbuf.reshape(kv_shape)
    return (y_prompt, y_sample, new_k, new_v)
```

```python
import functools

import jax
import jax.numpy as jnp
from jax import lax
from jax.experimental import pallas as pl
from jax.experimental.pallas import tpu as pltpu

D_MODEL = 1024
BATCH = 16
SEQ = 256
DEPTH = 4
DEC_BATCH = 2
DEC_SEQ = 1024
PAST_LEN = 256
GRID_W = 64
GRID_ROWS = DEC_SEQ // GRID_W
NA_HEADS = 8
NA_HEAD_DIM = 64
NA_WIDTH = NA_HEADS * NA_HEAD_DIM
NA_KH = 8
NA_KW = 16
POOL_WIDTH = 256
POOL_GROUPS = 4
POOL_GROUP_DIM = 64
CONV_WIDTH = 256
CONV_K = 31
N_BRANCH = 3
N_MIX = 3 * NA_WIDTH + POOL_WIDTH + 2 * CONV_WIDTH
N_GATE = N_BRANCH * D_MODEL
N_IN = N_MIX + N_GATE
FFN_HIDDEN = 2816
N_MOD = 6
EPS = 1e-6
NEG_INF = -1e30

N_PROMPT_TOK = BATCH * SEQ
N_SAMPLE_TOK = DEC_BATCH * DEC_SEQ
N_TOK = N_PROMPT_TOK + N_SAMPLE_TOK

LANES = 128
SUBLANES = 8
MOD_ROWS = 8
MOD_BLOCK_N = 1536
TOK_TILE = 256
ATT_BLOCK_ROWS = 4
BIAS_TILES = 32
POOL_PAD = 8
CONV_PAD = 16
MIX_CHUNK = 128
FFN_CHUNKS = ((0, 1024), (1024, 2048), (2048, 2816))
VMEM_LIMIT = 56 * 1024 * 1024

F32 = jnp.float32
BF16 = jnp.bfloat16

CAST_WEIGHTS = ((D_MODEL, N_IN), (NA_WIDTH, D_MODEL), (POOL_WIDTH, D_MODEL), (CONV_WIDTH, D_MODEL),
                (D_MODEL, D_MODEL), (D_MODEL, 2 * FFN_HIDDEN), (FFN_HIDDEN, D_MODEL))


def _sigmoid(x):
    return 1.0 / (1.0 + jnp.exp(-x))


def _rms(x):
    return x * lax.rsqrt(jnp.mean(x * x, axis=-1, keepdims=True) + EPS)


def _dot(a, b):
    return jnp.dot(a, b, preferred_element_type=F32)


def _dot_nt(a, b):
    return lax.dot_general(a, b, (((1,), (1,)), ((), ())), preferred_element_type=F32)


def _layer_spec(shape):
    zeros = (0,) * len(shape)
    return pl.BlockSpec((None,) + tuple(shape), lambda i, l: (l[0],) + zeros,
                        pipeline_mode=pl.Buffered(1))


def _whole_spec(shape):
    zeros = (0,) * len(shape)
    return pl.BlockSpec(tuple(shape), lambda i, l: zeros, pipeline_mode=pl.Buffered(1))


def _mod_kernel(cond_ref, w_ref, b_ref, out_ref):
    c = cond_ref[...]
    s = (c * _sigmoid(c)).astype(BF16)
    out_ref[...] = _dot(s, w_ref[...].astype(BF16)) + b_ref[...]


def _modulation(cond, w_mod, b_mod):
    n = N_MOD * D_MODEL
    return pl.pallas_call(
        _mod_kernel,
        grid=(DEPTH, n // MOD_BLOCK_N),
        in_specs=[
            pl.BlockSpec((MOD_ROWS, D_MODEL), lambda l, j: (0, 0)),
            pl.BlockSpec((None, D_MODEL, MOD_BLOCK_N), lambda l, j: (l, 0, j)),
            pl.BlockSpec((None, 1, MOD_BLOCK_N), lambda l, j: (l, 0, j)),
        ],
        out_specs=pl.BlockSpec((None, MOD_ROWS, MOD_BLOCK_N), lambda l, j: (l, 0, j)),
        out_shape=jax.ShapeDtypeStruct((DEPTH, MOD_ROWS, n), F32),
        compiler_params=pltpu.CompilerParams(
            dimension_semantics=("arbitrary", "arbitrary"), vmem_limit_bytes=VMEM_LIMIT),
        name="modulation",
    )(cond, w_mod, b_mod.reshape(DEPTH, 1, n))


def _bias_kernel(rpb_ref, out_ref):
    cq = lax.broadcasted_iota(jnp.int32, (GRID_W, LANES), 0)
    ln = lax.broadcasted_iota(jnp.int32, (GRID_W, LANES), 1)
    ck = jnp.where(ln < GRID_W, ln, ln - GRID_W)
    cs = jnp.clip(cq - NA_KW // 2, 0, GRID_W - NA_KW)
    col_in = (ck >= cs) & (ck < cs + NA_KW)
    low_half = ln < GRID_W
    neg = jnp.full((GRID_W, LANES), NEG_INF, F32)
    lo = GRID_W - NA_KW

    for head in range(NA_HEADS):
        rp = rpb_ref[head]
        lane = lax.broadcasted_iota(jnp.int32, rp.shape, 1)
        rolled = pltpu.roll(rp, lo, axis=1)
        first = rp[:, 0:1]
        last = rp[:, 2 * NA_KW - 2:2 * NA_KW - 1]
        ext = jnp.where(lane < lo, first, jnp.where(lane > lo + 2 * NA_KW - 2, last, rolled))

        def toeplitz(tile, shift):
            dr = tile - 15
            if dr < -(NA_KH - 1) or dr > NA_KH - 1:
                return neg
            row = jnp.broadcast_to(ext[dr + NA_KH - 1:dr + NA_KH, :], (GRID_W, LANES))
            return pltpu.roll(row, shift, axis=1, stride=1, stride_axis=0)

        for copy in range(2):
            for col in range(BIAS_TILES // 2):
                t_lo = 2 * col + copy
                t_hi = t_lo + 1
                a = toeplitz(t_lo, GRID_W + 1)
                b = toeplitz(t_hi, 1)
                pair = jnp.where(low_half, a, b)
                pair = jnp.where(col_in, pair, NEG_INF)
                out_ref[head, copy, :, col * LANES:(col + 1) * LANES] = pair


def _bias_tables(rpb):
    rp = jnp.pad(rpb, ((0, 0), (0, 0), (0, 1), (0, LANES - (2 * NA_KW - 1))))
    width = BIAS_TILES * GRID_W
    return pl.pallas_call(
        _bias_kernel,
        grid=(DEPTH,),
        in_specs=[pl.BlockSpec((None, NA_HEADS, 2 * NA_KH, LANES), lambda l: (l, 0, 0, 0))],
        out_specs=pl.BlockSpec((None, NA_HEADS, 2, GRID_W, width), lambda l: (l, 0, 0, 0, 0)),
        out_shape=jax.ShapeDtypeStruct((DEPTH, NA_HEADS, 2, GRID_W, width), F32),
        compiler_params=pltpu.CompilerParams(
            dimension_semantics=("arbitrary",), vmem_limit_bytes=VMEM_LIMIT),
        name="bias_tables",
    )(rp)


def _in_proj(x_ref, mod_ref, row, g1_ref, win_ref, u_ref, n_tok):
    shift1 = mod_ref[pl.ds(row, 1), 0:D_MODEL]
    scale1 = mod_ref[pl.ds(row, 1), D_MODEL:2 * D_MODEL]
    for t0 in range(0, n_tok, TOK_TILE):
        x = x_ref[t0:t0 + TOK_TILE, :]
        h = ((_rms(x) * g1_ref[...]) * (1.0 + scale1) + shift1).astype(BF16)
        u_ref[t0:t0 + TOK_TILE, :] = _dot(h, win_ref[...])


def _softmax_pv(parts):
    m = None
    for s, _ in parts:
        mi = jnp.max(s, axis=-1, keepdims=True)
        m = mi if m is None else jnp.maximum(m, mi)
    den = None
    acc = None
    for s, v in parts:
        e = jnp.exp(s - m)
        li = jnp.sum(e, axis=-1, keepdims=True)
        oi = _dot(e.astype(BF16), v)
        den = li if den is None else den + li
        acc = oi if acc is None else acc + oi
    return acc / den


def _context_attention(u_ref, kv_ref, y_ref, n_tok):
    scale = NA_HEAD_DIM ** -0.5
    lane = lax.broadcasted_iota(jnp.int32, (n_tok, LANES), 1)
    low = lane < NA_HEAD_DIM
    for hp in range(NA_HEADS // 2):
        c0 = hp * LANES
        qpair = u_ref[:, c0:c0 + LANES] * scale
        kpair = kv_ref[:, c0:c0 + LANES]
        vpair = kv_ref[:, NA_WIDTH + c0:NA_WIDTH + c0 + LANES]
        outs = []
        for hh in range(2):
            q = jnp.where(low if hh == 0 else ~low, qpair, 0.0).astype(BF16)
            outs.append(_softmax_pv([(_dot_nt(q, kpair), vpair)]))
        y_ref[:, c0:c0 + LANES] = jnp.where(low, outs[0], outs[1]).astype(BF16)


def _window_start(r):
    return min(max(r - NA_KH // 2, 0), GRID_ROWS - NA_KH)


def _neighbourhood_attention(u_ref, kv_ref, ctx_ref, bias_ref, y_ref):
    scale = NA_HEAD_DIM ** -0.5
    blk_tok = ATT_BLOCK_ROWS * GRID_W
    lane = lax.broadcasted_iota(jnp.int32, (blk_tok, LANES), 1)
    low = lane < NA_HEAD_DIM
    for blk in range(GRID_ROWS // ATT_BLOCK_ROWS):
        r0 = blk * ATT_BLOCK_ROWS
        first_row = _window_start(r0)
        key_rows = _window_start(r0 + ATT_BLOCK_ROWS - 1) + NA_KH - first_row
        key_rows += key_rows % 2
        j0 = min(first_row, GRID_ROWS - key_rows)
        key_tok = key_rows * GRID_W
        klane = lax.broadcasted_iota(jnp.int32, (GRID_W, key_tok), 1)
        q0 = r0 * GRID_W
        k0 = j0 * GRID_W
        for hp in range(NA_HEADS // 2):
            c0 = hp * LANES
            qpair = u_ref[q0:q0 + blk_tok, c0:c0 + LANES] * scale
            kpair = kv_ref[k0:k0 + key_tok, c0:c0 + LANES]
            vpair = kv_ref[k0:k0 + key_tok, NA_WIDTH + c0:NA_WIDTH + c0 + LANES]
            kctx = ctx_ref[:, c0:c0 + LANES]
            vctx = ctx_ref[:, NA_WIDTH + c0:NA_WIDTH + c0 + LANES]
            outs = []
            for hh in range(2):
                head = 2 * hp + hh
                q = jnp.where(low if hh == 0 else ~low, qpair, 0.0).astype(BF16)
                s_loc = _dot_nt(q, kpair)
                s_ctx = _dot_nt(q, kctx)
                strips = []
                for rr in range(ATT_BLOCK_ROWS):
                    r = r0 + rr
                    rs = _window_start(r)
                    tile0 = 15 + j0 - r
                    copy = tile0 % 2
                    b0 = (tile0 - copy) * GRID_W
                    bias = bias_ref[head, copy, :, b0:b0 + key_tok]
                    s = s_loc[rr * GRID_W:(rr + 1) * GRID_W, :] + bias
                    if key_rows > NA_KH:
                        valid = ((klane >= (rs - j0) * GRID_W)
                                 & (klane < (rs - j0 + NA_KH) * GRID_W))
                        s = jnp.where(valid, s, NEG_INF)
                    else:
                        assert rs == j0
                    strips.append(s)
                s_loc = jnp.concatenate(strips, axis=0)
                outs.append(_softmax_pv([(s_loc, vpair), (s_ctx, vctx)]))
            y_ref[q0:q0 + blk_tok, c0:c0 + LANES] = jnp.where(low, outs[0], outs[1]).astype(BF16)


def _pool_mixer(u_ref, pp_ref, wpool_ref, pscale_ref, y_ref, n_tok):
    c_in = 3 * NA_WIDTH
    zeros = jnp.zeros((POOL_PAD, POOL_WIDTH), F32)
    pp_ref[0:POOL_PAD, :] = zeros
    pp_ref[POOL_PAD + n_tok:2 * POOL_PAD + n_tok, :] = zeros
    pp_ref[POOL_PAD:POOL_PAD + n_tok, :] = u_ref[:, c_in:c_in + POOL_WIDTH]
    lane = lax.broadcasted_iota(jnp.int32, (MIX_CHUNK, LANES), 1)
    first = lane < POOL_GROUP_DIM
    tok = lax.broadcasted_iota(jnp.int32, (MIX_CHUNK, LANES), 0)
    wpool = wpool_ref[...].astype(BF16)

    for t0 in range(0, n_tok, MIX_CHUNK):
        def ld(d, half):
            a = POOL_PAD + t0 + d
            return pp_ref[a:a + MIX_CHUNK, half * LANES:(half + 1) * LANES]

        def window(half, lo, hi):
            acc = None
            for d in range(lo, hi):
                acc = ld(d, half) if acc is None else acc + ld(d, half)
            return acc

        a2 = window(0, -1, 1)
        a4 = a2 + ld(-2, 0) + ld(1, 0)
        a8 = window(1, -4, 4)
        a16 = a8 + window(1, -8, -4) + window(1, 4, 8)
        t = tok + t0
        halves = []
        for half, (small, big, ws, wb) in enumerate(((a2, a4, 1, 2), (a8, a16, 4, 8))):
            hw = jnp.where(first, ws, wb)
            cnt = (jnp.minimum(t + hw, n_tok) - jnp.maximum(t - hw, 0)).astype(F32)
            halves.append(jnp.where(first, small, big) / cnt - ld(0, half))
        pooled = jnp.concatenate(halves, axis=1).astype(BF16)
        y = _dot(pooled, wpool) * pscale_ref[...]
        y_ref[t0:t0 + MIX_CHUNK, NA_WIDTH:NA_WIDTH + POOL_WIDTH] = y.astype(BF16)


def _conv_mixer(u_ref, hp_ref, sh_ref, wdw_ref, bdw_ref, cng_ref, cnb_ref, y_ref, n_tok):
    c_a = 3 * NA_WIDTH + POOL_WIDTH
    c_g = c_a + CONV_WIDTH
    zeros = jnp.zeros((CONV_PAD, CONV_WIDTH), F32)
    hp_ref[0:CONV_PAD, :] = zeros
    hp_ref[CONV_PAD + n_tok:2 * CONV_PAD + n_tok, :] = zeros
    for t0 in range(0, n_tok, TOK_TILE):
        a = u_ref[t0:t0 + TOK_TILE, c_a:c_a + CONV_WIDTH]
        g = u_ref[t0:t0 + TOK_TILE, c_g:c_g + CONV_WIDTH]
        hp_ref[CONV_PAD + t0:CONV_PAD + t0 + TOK_TILE, :] = a * _sigmoid(g)
    off = CONV_PAD - CONV_K // 2
    span = MIX_CHUNK + (off + CONV_K - 1) // SUBLANES * SUBLANES
    for t0 in range(0, n_tok, MIX_CHUNK):
        for s in range(1, SUBLANES):
            sh_ref[s, 0:span, :] = hp_ref[t0 + s:t0 + s + span, :]
        acc = None
        for k in range(CONV_K):
            a, s = divmod(off + k, SUBLANES)
            if s == 0:
                win = hp_ref[t0 + a * SUBLANES:t0 + a * SUBLANES + MIX_CHUNK, :]
            else:
                win = sh_ref[s, a * SUBLANES:a * SUBLANES + MIX_CHUNK, :]
            term = wdw_ref[k:k + 1, :] * win
            acc = term if acc is None else acc + term
        acc = acc + bdw_ref[...]
        xc = acc - jnp.mean(acc, axis=-1, keepdims=True)
        y = xc * lax.rsqrt(jnp.mean(xc * xc, axis=-1, keepdims=True) + EPS)
        y = y * cng_ref[...] + cnb_ref[...]
        y_ref[t0:t0 + MIX_CHUNK, NA_WIDTH + POOL_WIDTH:D_MODEL] = (y * _sigmoid(y)).astype(BF16)


def _mix_prompt_kernel(layer_ref, x_ref, mod_ref, g1_ref, win_ref, wpool_ref, pscale_ref, wdw_ref,
                       bdw_ref, cng_ref, cnb_ref,
                       cin_ref, coa_ref, cob_ref, coc_ref, cout_ref, cgu_ref, cdown_ref,
                       kbuf_ref, vbuf_ref, y_ref, k_ref, v_ref, wmix_ref,
                       bg_ref, boa_ref, bob_ref, boc_ref, bout_ref, bgu_ref, bdown_ref,
                       u_ref, kv_ref, pp_ref, hp_ref, sh_ref):
    del layer_ref, kbuf_ref, vbuf_ref

    @pl.when(pl.program_id(0) == 0)
    def _():
        wmix_ref[...] = win_ref[...].astype(BF16)

    bg_ref[...] = cin_ref[:, N_MIX:].astype(BF16)
    for src, dst in ((coa_ref, boa_ref), (cob_ref, bob_ref), (coc_ref, boc_ref),
                     (cout_ref, bout_ref), (cgu_ref, bgu_ref), (cdown_ref, bdown_ref)):
        dst[...] = src[...].astype(BF16)

    _in_proj(x_ref, mod_ref, 0, g1_ref, wmix_ref, u_ref, SEQ)
    k_ref[...] = u_ref[:, NA_WIDTH:2 * NA_WIDTH]
    v_ref[...] = u_ref[:, 2 * NA_WIDTH:3 * NA_WIDTH]
    kv_ref[...] = u_ref[:, NA_WIDTH:3 * NA_WIDTH].astype(BF16)
    _context_attention(u_ref, kv_ref, y_ref, SEQ)
    _pool_mixer(u_ref, pp_ref, wpool_ref, pscale_ref, y_ref, SEQ)
    _conv_mixer(u_ref, hp_ref, sh_ref, wdw_ref, bdw_ref, cng_ref, cnb_ref, y_ref, SEQ)


def _mix_sample_kernel(layer_ref, x_ref, mod_ref, g1_ref, wmix_ref, wpool_ref, pscale_ref, wdw_ref,
                       bdw_ref, cng_ref, cnb_ref, ck_ref, cv_ref, bias_ref, y_ref,
                       u_ref, kv_ref, ctx_ref, pp_ref, hp_ref, sh_ref):
    del layer_ref
    row = 1 + pl.program_id(0)
    _in_proj(x_ref, mod_ref, row, g1_ref, wmix_ref, u_ref, DEC_SEQ)
    kv_ref[...] = u_ref[:, NA_WIDTH:3 * NA_WIDTH].astype(BF16)
    ctx_ref[:, 0:NA_WIDTH] = ck_ref[...].astype(BF16)
    ctx_ref[:, NA_WIDTH:2 * NA_WIDTH] = cv_ref[...].astype(BF16)
    _neighbourhood_attention(u_ref, kv_ref, ctx_ref, bias_ref, y_ref)
    _pool_mixer(u_ref, pp_ref, wpool_ref, pscale_ref, y_ref, DEC_SEQ)
    _conv_mixer(u_ref, hp_ref, sh_ref, wdw_ref, bdw_ref, cng_ref, cnb_ref, y_ref, DEC_SEQ)


def _small_weight_specs():
    return [_layer_spec((POOL_WIDTH, POOL_WIDTH)), _layer_spec((1, POOL_WIDTH)),
            _layer_spec((CONV_K, CONV_WIDTH)), _layer_spec((1, CONV_WIDTH)),
            _layer_spec((1, CONV_WIDTH)), _layer_spec((1, CONV_WIDTH))]


def _mix_scratch(n_tok):
    return [
        pltpu.VMEM((n_tok, N_MIX), F32),
        pltpu.VMEM((n_tok, 2 * NA_WIDTH), BF16),
    ], [
        pltpu.VMEM((n_tok + 2 * POOL_PAD, POOL_WIDTH), F32),
        pltpu.VMEM((n_tok + 2 * CONV_PAD, CONV_WIDTH), F32),
        pltpu.VMEM((SUBLANES, MIX_CHUNK + 2 * CONV_PAD, CONV_WIDTH), F32),
    ]


def _mix_prompt(layer, x_all, mods, g1, w_in, small, cast_w, kbuf, vbuf):
    head, tail = _mix_scratch(SEQ)
    operands = (layer, x_all, mods, g1, w_in, *small, *cast_w, kbuf, vbuf)
    n_in = len(operands)
    chunk_specs, bf_specs, bf_shapes = [], [], []
    for rows, cols in CAST_WEIGHTS:
        r = rows // BATCH
        chunk_specs.append(pl.BlockSpec((None, r, cols), lambda b, l: (l[0], b, 0)))
        out_cols = N_GATE if cols == N_IN else cols
        bf_specs.append(pl.BlockSpec((r, out_cols), lambda b, l: (b, 0)))
        bf_shapes.append(jax.ShapeDtypeStruct((rows, out_cols), BF16))
    grid_spec = pltpu.PrefetchScalarGridSpec(
        num_scalar_prefetch=1,
        grid=(BATCH,),
        in_specs=[pl.BlockSpec((SEQ, D_MODEL), lambda b, l: (b, 0)),
                  _layer_spec((MOD_ROWS, N_MOD * D_MODEL)),
                  _layer_spec((1, D_MODEL)),
                  _layer_spec((D_MODEL, N_MIX))] + _small_weight_specs() + chunk_specs
        + [pl.BlockSpec(memory_space=pl.ANY), pl.BlockSpec(memory_space=pl.ANY)],
        out_specs=[pl.BlockSpec((SEQ, D_MODEL), lambda b, l: (b, 0)),
                   pl.BlockSpec((None, None, SEQ, NA_WIDTH), lambda b, l: (b, l[0], 0, 0)),
                   pl.BlockSpec((None, None, SEQ, NA_WIDTH), lambda b, l: (b, l[0], 0, 0)),
                   pl.BlockSpec((D_MODEL, N_MIX), lambda b, l: (0, 0))] + bf_specs,
        scratch_shapes=head + tail,
    )
    return pl.pallas_call(
        _mix_prompt_kernel,
        grid_spec=grid_spec,
        out_shape=[jax.ShapeDtypeStruct((N_PROMPT_TOK, D_MODEL), BF16),
                   jax.ShapeDtypeStruct(kbuf.shape, F32),
                   jax.ShapeDtypeStruct(vbuf.shape, F32),
                   jax.ShapeDtypeStruct((D_MODEL, N_MIX), BF16)] + bf_shapes,
        input_output_aliases={n_in - 2: 1, n_in - 1: 2},
        compiler_params=pltpu.CompilerParams(
            dimension_semantics=("arbitrary",), vmem_limit_bytes=VMEM_LIMIT),
        name="mix_prompt",
    )(*operands)


def _mix_sample(layer, x_all, mods, g1, wmix, small, cache_k, cache_v, bias):
    head, tail = _mix_scratch(DEC_SEQ)
    first_blk = N_PROMPT_TOK // DEC_SEQ
    ctx_spec = pl.BlockSpec((None, None, PAST_LEN, NA_WIDTH), lambda b, l: (b, l[0], 0, 0))
    grid_spec = pltpu.PrefetchScalarGridSpec(
        num_scalar_prefetch=1,
        grid=(DEC_BATCH,),
        in_specs=[pl.BlockSpec((DEC_SEQ, D_MODEL), lambda b, l: (first_blk + b, 0)),
                  _layer_spec((MOD_ROWS, N_MOD * D_MODEL)),
                  _layer_spec((1, D_MODEL)),
                  _whole_spec((D_MODEL, N_MIX))] + _small_weight_specs()
        + [ctx_spec, ctx_spec, _layer_spec((NA_HEADS, 2, GRID_W, BIAS_TILES * GRID_W))],
        out_specs=pl.BlockSpec((DEC_SEQ, D_MODEL), lambda b, l: (b, 0)),
        scratch_shapes=head + [pltpu.VMEM((PAST_LEN, 2 * NA_WIDTH), BF16)] + tail,
    )
    return pl.pallas_call(
        _mix_sample_kernel,
        grid_spec=grid_spec,
        out_shape=jax.ShapeDtypeStruct((N_SAMPLE_TOK, D_MODEL), BF16),
        compiler_params=pltpu.CompilerParams(
            dimension_semantics=("arbitrary",), vmem_limit_bytes=VMEM_LIMIT),
        name="mix_sample",
    )(layer, x_all, mods, g1, wmix, *small, cache_k, cache_v, bias)


FFN_TILE = 2 * TOK_TILE
PROMPT_TILES = N_PROMPT_TOK // FFN_TILE
SAMPLE_TILES = N_SAMPLE_TOK // FFN_TILE


def _ffn_kernel(final, layer_ref, x_ref, yp_ref, ys_ref, mod_ref, g1_ref, g2_ref, fg_ref, bg_ref,
                wg_ref, woa_ref, wob_ref, woc_ref, wout_ref, wgu_ref, wdown_ref, o_ref):
    del layer_ref
    i = pl.program_id(0)
    is_prompt = i < PROMPT_TILES
    row = jnp.where(is_prompt, 0, 1 + (i - PROMPT_TILES) // (DEC_SEQ // FFN_TILE))

    def mod(j):
        return mod_ref[pl.ds(row, 1), j * D_MODEL:(j + 1) * D_MODEL]

    branches = ((0, NA_WIDTH, woa_ref), (NA_WIDTH, NA_WIDTH + POOL_WIDTH, wob_ref),
                (NA_WIDTH + POOL_WIDTH, D_MODEL, woc_ref))
    halves = [slice(t0, t0 + TOK_TILE) for t0 in range(0, FFN_TILE, TOK_TILE)]
    xs = [x_ref[rows, :] for rows in halves]
    hs = [((_rms(x) * g1_ref[...]) * (1.0 + mod(1)) + mod(0)).astype(BF16) for x in xs]
    merged = [None] * len(halves)
    for br, (c0, c1, w_ref) in enumerate(branches):
        g0 = br * D_MODEL
        gates = [_sigmoid(_dot(h, wg_ref[:, g0:g0 + D_MODEL]) + bg_ref[:, g0:g0 + D_MODEL])
                 for h in hs]
        ys = [jnp.where(is_prompt, yp_ref[rows, c0:c1], ys_ref[rows, c0:c1]) for rows in halves]
        terms = [gate * _dot(y, w_ref[...]) for gate, y in zip(gates, ys)]
        merged = [t if m is None else m + t for m, t in zip(merged, terms)]
    x1s = [x + mod(2) * _dot(m.astype(BF16), wout_ref[...]) for x, m in zip(xs, merged)]
    h2s = [((_rms(x1) * g2_ref[...]) * (1.0 + mod(4)) + mod(3)).astype(BF16) for x1 in x1s]
    accs = [None] * len(halves)
    for c0, c1 in FFN_CHUNKS:
        a_s = [_dot(h2, wgu_ref[:, c0:c1]) for h2 in h2s]
        b_s = [_dot(h2, wgu_ref[:, FFN_HIDDEN + c0:FFN_HIDDEN + c1]) for h2 in h2s]
        ts = [((a * _sigmoid(a)) * b).astype(BF16) for a, b in zip(a_s, b_s)]
        ds = [_dot(t, wdown_ref[c0:c1, :]) for t in ts]
        accs = [d if acc is None else acc + d for acc, d in zip(accs, ds)]
    for rows, x1, acc in zip(halves, x1s, accs):
        x2 = x1 + mod(5) * acc
        if final:
            x2 = _rms(x2) * fg_ref[...]
        o_ref[rows, :] = x2


def _ffn(layer, x_all, y_p, y_s, mods, g1, g2, fg, b_gate, weights, final):
    tile = pl.BlockSpec((FFN_TILE, D_MODEL), lambda i, l: (i, 0))
    yp_spec = pl.BlockSpec((FFN_TILE, D_MODEL), lambda i, l: (jnp.minimum(i, PROMPT_TILES - 1), 0))
    ys_spec = pl.BlockSpec((FFN_TILE, D_MODEL), lambda i, l: (jnp.maximum(i - PROMPT_TILES, 0), 0))
    grid_spec = pltpu.PrefetchScalarGridSpec(
        num_scalar_prefetch=1,
        grid=(N_TOK // FFN_TILE,),
        in_specs=[tile, yp_spec, ys_spec,
                  _layer_spec((MOD_ROWS, N_MOD * D_MODEL)),
                  _layer_spec((1, D_MODEL)), _layer_spec((1, D_MODEL)), _whole_spec((1, D_MODEL)),
                  _layer_spec((1, N_GATE)),
                  _whole_spec((D_MODEL, N_GATE)),
                  _whole_spec((NA_WIDTH, D_MODEL)), _whole_spec((POOL_WIDTH, D_MODEL)),
                  _whole_spec((CONV_WIDTH, D_MODEL)), _whole_spec((D_MODEL, D_MODEL)),
                  _whole_spec((D_MODEL, 2 * FFN_HIDDEN)), _whole_spec((FFN_HIDDEN, D_MODEL))],
        out_specs=tile,
    )
    return pl.pallas_call(
        functools.partial(_ffn_kernel, final),
        grid_spec=grid_spec,
        out_shape=jax.ShapeDtypeStruct((N_TOK, D_MODEL), F32),
        compiler_params=pltpu.CompilerParams(
            dimension_semantics=("arbitrary",), vmem_limit_bytes=VMEM_LIMIT),
        name="merge_ffn",
    )(layer, x_all, y_p, y_s, mods, g1, g2, fg, b_gate, *weights)


def _block_diag(w_pool):
    out = jnp.zeros((DEPTH, POOL_WIDTH, POOL_WIDTH), w_pool.dtype)
    for g in range(POOL_GROUPS):
        a = g * POOL_GROUP_DIM
        out = out.at[:, a:a + POOL_GROUP_DIM, a:a + POOL_GROUP_DIM].set(w_pool[:, g])
    return out


def kernel(x_prompt, x_sample, cache_k, cache_v, c, c_ctx, w_mod, b_mod, norm1_g, norm2_g, w_in,
           b_gate, rpb, w_oa, w_pool, pool_scale, w_ob, w_dw, b_dw, conv_norm_g, conv_norm_b,
           w_oc, w_out, w_gu, w_down, final_g):
    x_all = jnp.concatenate([x_prompt.reshape(N_PROMPT_TOK, D_MODEL),
                             x_sample.reshape(N_SAMPLE_TOK, D_MODEL)], axis=0)
    cond = jnp.concatenate([c_ctx[None, :], c,
                            jnp.zeros((MOD_ROWS - 1 - DEC_BATCH, D_MODEL), F32)], axis=0)
    mods = _modulation(cond, w_mod, b_mod)
    bias = _bias_tables(rpb)
    ck = cache_k.reshape(DEC_BATCH, DEPTH, PAST_LEN, NA_WIDTH)
    cv = cache_v.reshape(DEC_BATCH, DEPTH, PAST_LEN, NA_WIDTH)
    g1 = norm1_g.reshape(DEPTH, 1, D_MODEL)
    g2 = norm2_g.reshape(DEPTH, 1, D_MODEL)
    fg = final_g.reshape(1, D_MODEL)
    bg = b_gate.reshape(DEPTH, 1, N_GATE)
    small = (_block_diag(w_pool), pool_scale.reshape(DEPTH, 1, POOL_WIDTH), w_dw,
             b_dw.reshape(DEPTH, 1, CONV_WIDTH), conv_norm_g.reshape(DEPTH, 1, CONV_WIDTH),
             conv_norm_b.reshape(DEPTH, 1, CONV_WIDTH))
    cast_w = (w_in, w_oa, w_ob, w_oc, w_out, w_gu, w_down)
    kbuf = jnp.zeros((BATCH, DEPTH, SEQ, NA_WIDTH), F32)
    vbuf = jnp.zeros((BATCH, DEPTH, SEQ, NA_WIDTH), F32)
    for l in range(DEPTH):
        layer = jnp.full((1,), l, jnp.int32)
        y_p, kbuf, vbuf, wmix, *ffn_w = _mix_prompt(layer, x_all, mods, g1, w_in, small, cast_w,
                                                    kbuf, vbuf)
        y_s = _mix_sample(layer, x_all, mods, g1, wmix, small, ck, cv, bias)
        x_all = _ffn(layer, x_all, y_p, y_s, mods, g1, g2, fg, bg, ffn_w, l == DEPTH - 1)
    y_prompt = x_all[:N_PROMPT_TOK].reshape(BATCH, SEQ, D_MODEL)
    y_sample = x_all[N_PROMPT_TOK:].reshape(DEC_BATCH, DEC_SEQ, D_MODEL)
    kv_shape = (BATCH, DEPTH, SEQ, NA_HEADS, NA_HEAD_DIM)
    new_k = kbuf.reshape(kv_shape)
    new_v = vbuf.reshape(kv_shape)
    return (y_prompt, y_sample, new_k, new_v)
```

```python
import functools

import jax
import jax.numpy as jnp
from jax import lax
from jax.experimental import pallas as pl
from jax.experimental.pallas import tpu as pltpu

D_MODEL = 1024
BATCH = 16
SEQ = 256
DEPTH = 4
DEC_BATCH = 2
DEC_SEQ = 1024
PAST_LEN = 256
GRID_W = 64
GRID_ROWS = DEC_SEQ // GRID_W
NA_HEADS = 8
NA_HEAD_DIM = 64
NA_WIDTH = NA_HEADS * NA_HEAD_DIM
NA_KH = 8
NA_KW = 16
POOL_WIDTH = 256
POOL_GROUPS = 4
POOL_GROUP_DIM = 64
CONV_WIDTH = 256
CONV_K = 31
N_BRANCH = 3
N_MIX = 3 * NA_WIDTH + POOL_WIDTH + 2 * CONV_WIDTH
N_GATE = N_BRANCH * D_MODEL
N_IN = N_MIX + N_GATE
FFN_HIDDEN = 2816
N_MOD = 6
EPS = 1e-6
NEG_INF = -1e30

N_PROMPT_TOK = BATCH * SEQ
N_SAMPLE_TOK = DEC_BATCH * DEC_SEQ
N_TOK = N_PROMPT_TOK + N_SAMPLE_TOK

LANES = 128
SUBLANES = 8
MOD_ROWS = 8
MOD_BLOCK_N = 1536
TOK_TILE = 256
ATT_BLOCK_ROWS = 4
BIAS_TILES = 16
POOL_PAD = 8
CONV_PAD = 16
MIX_CHUNK = 128
FFN_CHUNKS = ((0, 1024), (1024, 2048), (2048, 2816))
VMEM_LIMIT = 56 * 1024 * 1024

F32 = jnp.float32
BF16 = jnp.bfloat16

CAST_WEIGHTS = ((D_MODEL, N_IN), (NA_WIDTH, D_MODEL), (POOL_WIDTH, D_MODEL), (CONV_WIDTH, D_MODEL),
                (D_MODEL, D_MODEL), (D_MODEL, 2 * FFN_HIDDEN), (FFN_HIDDEN, D_MODEL))


def _sigmoid(x):
    return 1.0 / (1.0 + jnp.exp(-x))


def _rms(x):
    return x * lax.rsqrt(jnp.mean(x * x, axis=-1, keepdims=True) + EPS)


def _dot(a, b):
    return jnp.dot(a, b, preferred_element_type=F32)


def _dot_nt(a, b):
    return lax.dot_general(a, b, (((1,), (1,)), ((), ())), preferred_element_type=F32)


def _layer_spec(shape):
    zeros = (0,) * len(shape)
    return pl.BlockSpec((None,) + tuple(shape), lambda i, l: (l[0],) + zeros,
                        pipeline_mode=pl.Buffered(1))


def _whole_spec(shape):
    zeros = (0,) * len(shape)
    return pl.BlockSpec(tuple(shape), lambda i, l: zeros, pipeline_mode=pl.Buffered(1))


def _mod_kernel(cond_ref, w_ref, b_ref, out_ref):
    c = cond_ref[...]
    s = (c * _sigmoid(c)).astype(BF16)
    out_ref[...] = _dot(s, w_ref[...].astype(BF16)) + b_ref[...]


def _modulation(cond, w_mod, b_mod):
    n = N_MOD * D_MODEL
    return pl.pallas_call(
        _mod_kernel,
        grid=(DEPTH, n // MOD_BLOCK_N),
        in_specs=[
            pl.BlockSpec((MOD_ROWS, D_MODEL), lambda l, j: (0, 0)),
            pl.BlockSpec((None, D_MODEL, MOD_BLOCK_N), lambda l, j: (l, 0, j)),
            pl.BlockSpec((None, 1, MOD_BLOCK_N), lambda l, j: (l, 0, j)),
        ],
        out_specs=pl.BlockSpec((None, MOD_ROWS, MOD_BLOCK_N), lambda l, j: (l, 0, j)),
        out_shape=jax.ShapeDtypeStruct((DEPTH, MOD_ROWS, n), F32),
        compiler_params=pltpu.CompilerParams(
            dimension_semantics=("arbitrary", "arbitrary"), vmem_limit_bytes=VMEM_LIMIT),
        name="modulation",
    )(cond, w_mod, b_mod.reshape(DEPTH, 1, n))


def _bias_kernel(rpb_ref, out_ref):
    cq = lax.broadcasted_iota(jnp.int32, (GRID_W, LANES), 0)
    ln = lax.broadcasted_iota(jnp.int32, (GRID_W, LANES), 1)
    ck = jnp.where(ln < GRID_W, ln, ln - GRID_W)
    cs = jnp.clip(cq - NA_KW // 2, 0, GRID_W - NA_KW)
    col_in = (ck >= cs) & (ck < cs + NA_KW)
    low_half = ln < GRID_W
    neg = jnp.full((GRID_W, LANES), NEG_INF, F32)
    lo = GRID_W - NA_KW

    for head in range(NA_HEADS):
        rp = rpb_ref[head]
        lane = lax.broadcasted_iota(jnp.int32, rp.shape, 1)
        rolled = pltpu.roll(rp, lo, axis=1)
        first = rp[:, 0:1]
        last = rp[:, 2 * NA_KW - 2:2 * NA_KW - 1]
        ext = jnp.where(lane < lo, first, jnp.where(lane > lo + 2 * NA_KW - 2, last, rolled))

        def toeplitz(tile, shift):
            dr = tile - (NA_KH - 1)
            if dr < -(NA_KH - 1) or dr > NA_KH - 1:
                return neg
            row = jnp.broadcast_to(ext[dr + NA_KH - 1:dr + NA_KH, :], (GRID_W, LANES))
            return pltpu.roll(row, shift, axis=1, stride=1, stride_axis=0)

        for copy in range(2):
            for col in range(BIAS_TILES // 2):
                t_lo = 2 * col + copy
                t_hi = t_lo + 1
                a = toeplitz(t_lo, GRID_W + 1)
                b = toeplitz(t_hi, 1)
                pair = jnp.where(low_half, a, b)
                pair = jnp.where(col_in, pair, NEG_INF)
                out_ref[head, copy, :, col * LANES:(col + 1) * LANES] = pair


def _bias_tables(rpb):
    rp = jnp.pad(rpb, ((0, 0), (0, 0), (0, 1), (0, LANES - (2 * NA_KW - 1))))
    width = BIAS_TILES * GRID_W
    return pl.pallas_call(
        _bias_kernel,
        grid=(DEPTH,),
        in_specs=[pl.BlockSpec((None, NA_HEADS, 2 * NA_KH, LANES), lambda l: (l, 0, 0, 0))],
        out_specs=pl.BlockSpec((None, NA_HEADS, 2, GRID_W, width), lambda l: (l, 0, 0, 0, 0)),
        out_shape=jax.ShapeDtypeStruct((DEPTH, NA_HEADS, 2, GRID_W, width), F32),
        compiler_params=pltpu.CompilerParams(
            dimension_semantics=("arbitrary",), vmem_limit_bytes=VMEM_LIMIT),
        name="bias_tables",
    )(rp)


def _in_proj(x_ref, mod_ref, row, g1_ref, win_ref, u_ref, n_tok):
    shift1 = mod_ref[pl.ds(row, 1), 0:D_MODEL]
    scale1 = mod_ref[pl.ds(row, 1), D_MODEL:2 * D_MODEL]
    for t0 in range(0, n_tok, TOK_TILE):
        x = x_ref[t0:t0 + TOK_TILE, :]
        h = ((_rms(x) * g1_ref[...]) * (1.0 + scale1) + shift1).astype(BF16)
        u_ref[t0:t0 + TOK_TILE, :] = _dot(h, win_ref[...])


def _softmax_pv(parts):
    m = None
    for s, _ in parts:
        mi = jnp.max(s, axis=-1, keepdims=True)
        m = mi if m is None else jnp.maximum(m, mi)
    den = None
    acc = None
    for s, v in parts:
        e = jnp.exp(s - m)
        li = jnp.sum(e, axis=-1, keepdims=True)
        oi = _dot(e.astype(BF16), v)
        den = li if den is None else den + li
        acc = oi if acc is None else acc + oi
    return acc / den


def _context_attention(u_ref, kv_ref, y_ref, n_tok):
    scale = NA_HEAD_DIM ** -0.5
    lane = lax.broadcasted_iota(jnp.int32, (n_tok, LANES), 1)
    low = lane < NA_HEAD_DIM
    for hp in range(NA_HEADS // 2):
        c0 = hp * LANES
        qpair = u_ref[:, c0:c0 + LANES] * scale
        kpair = kv_ref[:, c0:c0 + LANES]
        vpair = kv_ref[:, NA_WIDTH + c0:NA_WIDTH + c0 + LANES]
        outs = []
        for hh in range(2):
            q = jnp.where(low if hh == 0 else ~low, qpair, 0.0).astype(BF16)
            outs.append(_softmax_pv([(_dot_nt(q, kpair), vpair)]))
        y_ref[:, c0:c0 + LANES] = jnp.where(low, outs[0], outs[1]).astype(BF16)


def _window_start(r):
    return min(max(r - NA_KH // 2, 0), GRID_ROWS - NA_KH)


def _neighbourhood_attention(u_ref, kv_ref, ctx_ref, bias_ref, y_ref):
    scale = NA_HEAD_DIM ** -0.5
    blk_tok = ATT_BLOCK_ROWS * GRID_W
    lane = lax.broadcasted_iota(jnp.int32, (blk_tok, LANES), 1)
    low = lane < NA_HEAD_DIM
    for blk in range(GRID_ROWS // ATT_BLOCK_ROWS):
        r0 = blk * ATT_BLOCK_ROWS
        first_row = _window_start(r0)
        key_rows = _window_start(r0 + ATT_BLOCK_ROWS - 1) + NA_KH - first_row
        key_rows += key_rows % 2
        j0 = min(first_row, GRID_ROWS - key_rows)
        key_tok = key_rows * GRID_W
        klane = lax.broadcasted_iota(jnp.int32, (GRID_W, key_tok), 1)
        q0 = r0 * GRID_W
        k0 = j0 * GRID_W
        for hp in range(NA_HEADS // 2):
            c0 = hp * LANES
            qpair = u_ref[q0:q0 + blk_tok, c0:c0 + LANES] * scale
            kpair = kv_ref[k0:k0 + key_tok, c0:c0 + LANES]
            vpair = kv_ref[k0:k0 + key_tok, NA_WIDTH + c0:NA_WIDTH + c0 + LANES]
            kctx = ctx_ref[:, c0:c0 + LANES]
            vctx = ctx_ref[:, NA_WIDTH + c0:NA_WIDTH + c0 + LANES]
            outs = []
            for hh in range(2):
                head = 2 * hp + hh
                q = jnp.where(low if hh == 0 else ~low, qpair, 0.0).astype(BF16)
                s_loc = _dot_nt(q, kpair)
                s_ctx = _dot_nt(q, kctx)
                strips = []
                for rr in range(ATT_BLOCK_ROWS):
                    r = r0 + rr
                    rs = _window_start(r)
                    tile0 = NA_KH - 1 + j0 - r
                    assert 0 <= tile0 and tile0 + key_rows <= BIAS_TILES
                    copy = tile0 % 2
                    b0 = (tile0 - copy) * GRID_W
                    bias = bias_ref[head, copy, :, b0:b0 + key_tok]
                    s = s_loc[rr * GRID_W:(rr + 1) * GRID_W, :] + bias
                    if key_rows > NA_KH:
                        valid = ((klane >= (rs - j0) * GRID_W)
                                 & (klane < (rs - j0 + NA_KH) * GRID_W))
                        s = jnp.where(valid, s, NEG_INF)
                    else:
                        assert rs == j0
                    strips.append(s)
                s_loc = jnp.concatenate(strips, axis=0)
                outs.append(_softmax_pv([(s_loc, vpair), (s_ctx, vctx)]))
            y_ref[q0:q0 + blk_tok, c0:c0 + LANES] = jnp.where(low, outs[0], outs[1]).astype(BF16)


def _pool_mixer(u_ref, pp_ref, wpool_ref, pscale_ref, y_ref, n_tok):
    c_in = 3 * NA_WIDTH
    zeros = jnp.zeros((POOL_PAD, POOL_WIDTH), F32)
    pp_ref[0:POOL_PAD, :] = zeros
    pp_ref[POOL_PAD + n_tok:2 * POOL_PAD + n_tok, :] = zeros
    pp_ref[POOL_PAD:POOL_PAD + n_tok, :] = u_ref[:, c_in:c_in + POOL_WIDTH]
    lane = lax.broadcasted_iota(jnp.int32, (MIX_CHUNK, LANES), 1)
    first = lane < POOL_GROUP_DIM
    tok = lax.broadcasted_iota(jnp.int32, (MIX_CHUNK, LANES), 0)
    wpool = wpool_ref[...].astype(BF16)

    for t0 in range(0, n_tok, MIX_CHUNK):
        def ld(d, half):
            a = POOL_PAD + t0 + d
            return pp_ref[a:a + MIX_CHUNK, half * LANES:(half + 1) * LANES]

        def window(half, lo, hi):
            acc = None
            for d in range(lo, hi):
                acc = ld(d, half) if acc is None else acc + ld(d, half)
            return acc

        a2 = window(0, -1, 1)
        a4 = a2 + ld(-2, 0) + ld(1, 0)
        a8 = window(1, -4, 4)
        a16 = a8 + window(1, -8, -4) + window(1, 4, 8)
        t = tok + t0
        halves = []
        for half, (small, big, ws, wb) in enumerate(((a2, a4, 1, 2), (a8, a16, 4, 8))):
            hw = jnp.where(first, ws, wb)
            cnt = (jnp.minimum(t + hw, n_tok) - jnp.maximum(t - hw, 0)).astype(F32)
            halves.append(jnp.where(first, small, big) / cnt - ld(0, half))
        pooled = jnp.concatenate(halves, axis=1).astype(BF16)
        y = _dot(pooled, wpool) * pscale_ref[...]
        y_ref[t0:t0 + MIX_CHUNK, NA_WIDTH:NA_WIDTH + POOL_WIDTH] = y.astype(BF16)


def _conv_mixer(u_ref, hp_ref, sh_ref, wdw_ref, bdw_ref, cng_ref, cnb_ref, y_ref, n_tok):
    c_a = 3 * NA_WIDTH + POOL_WIDTH
    c_g = c_a + CONV_WIDTH
    zeros = jnp.zeros((CONV_PAD, CONV_WIDTH), F32)
    hp_ref[0:CONV_PAD, :] = zeros
    hp_ref[CONV_PAD + n_tok:2 * CONV_PAD + n_tok, :] = zeros
    for t0 in range(0, n_tok, TOK_TILE):
        a = u_ref[t0:t0 + TOK_TILE, c_a:c_a + CONV_WIDTH]
        g = u_ref[t0:t0 + TOK_TILE, c_g:c_g + CONV_WIDTH]
        hp_ref[CONV_PAD + t0:CONV_PAD + t0 + TOK_TILE, :] = a * _sigmoid(g)
    off = CONV_PAD - CONV_K // 2
    span = MIX_CHUNK + (off + CONV_K - 1) // SUBLANES * SUBLANES
    for t0 in range(0, n_tok, MIX_CHUNK):
        for s in range(1, SUBLANES):
            sh_ref[s, 0:span, :] = hp_ref[t0 + s:t0 + s + span, :]
        acc = None
        for k in range(CONV_K):
            a, s = divmod(off + k, SUBLANES)
            if s == 0:
                win = hp_ref[t0 + a * SUBLANES:t0 + a * SUBLANES + MIX_CHUNK, :]
            else:
                win = sh_ref[s, a * SUBLANES:a * SUBLANES + MIX_CHUNK, :]
            term = wdw_ref[k:k + 1, :] * win
            acc = term if acc is None else acc + term
        acc = acc + bdw_ref[...]
        xc = acc - jnp.mean(acc, axis=-1, keepdims=True)
        y = xc * lax.rsqrt(jnp.mean(xc * xc, axis=-1, keepdims=True) + EPS)
        y = y * cng_ref[...] + cnb_ref[...]
        y_ref[t0:t0 + MIX_CHUNK, NA_WIDTH + POOL_WIDTH:D_MODEL] = (y * _sigmoid(y)).astype(BF16)


def _mix_prompt_kernel(layer_ref, x_ref, mod_ref, g1_ref, win_ref, wpool_ref, pscale_ref, wdw_ref,
                       bdw_ref, cng_ref, cnb_ref,
                       cin_ref, coa_ref, cob_ref, coc_ref, cout_ref, cgu_ref, cdown_ref,
                       kbuf_ref, vbuf_ref, y_ref, k_ref, v_ref, wmix_ref,
                       bg_ref, boa_ref, bob_ref, boc_ref, bout_ref, bgu_ref, bdown_ref,
                       u_ref, kv_ref, pp_ref, hp_ref, sh_ref):
    del layer_ref, kbuf_ref, vbuf_ref

    @pl.when(pl.program_id(0) == 0)
    def _():
        wmix_ref[...] = win_ref[...].astype(BF16)

    bg_ref[...] = cin_ref[:, N_MIX:].astype(BF16)
    for src, dst in ((coa_ref, boa_ref), (cob_ref, bob_ref), (coc_ref, boc_ref),
                     (cout_ref, bout_ref), (cgu_ref, bgu_ref), (cdown_ref, bdown_ref)):
        dst[...] = src[...].astype(BF16)

    _in_proj(x_ref, mod_ref, 0, g1_ref, wmix_ref, u_ref, SEQ)
    k_ref[...] = u_ref[:, NA_WIDTH:2 * NA_WIDTH]
    v_ref[...] = u_ref[:, 2 * NA_WIDTH:3 * NA_WIDTH]
    kv_ref[...] = u_ref[:, NA_WIDTH:3 * NA_WIDTH].astype(BF16)
    _context_attention(u_ref, kv_ref, y_ref, SEQ)
    _pool_mixer(u_ref, pp_ref, wpool_ref, pscale_ref, y_ref, SEQ)
    _conv_mixer(u_ref, hp_ref, sh_ref, wdw_ref, bdw_ref, cng_ref, cnb_ref, y_ref, SEQ)


def _mix_sample_kernel(layer_ref, x_ref, mod_ref, g1_ref, wmix_ref, wpool_ref, pscale_ref, wdw_ref,
                       bdw_ref, cng_ref, cnb_ref, ck_ref, cv_ref, bias_ref, y_ref,
                       u_ref, kv_ref, ctx_ref, pp_ref, hp_ref, sh_ref):
    del layer_ref
    row = 1 + pl.program_id(0)
    _in_proj(x_ref, mod_ref, row, g1_ref, wmix_ref, u_ref, DEC_SEQ)
    kv_ref[...] = u_ref[:, NA_WIDTH:3 * NA_WIDTH].astype(BF16)
    ctx_ref[:, 0:NA_WIDTH] = ck_ref[...].astype(BF16)
    ctx_ref[:, NA_WIDTH:2 * NA_WIDTH] = cv_ref[...].astype(BF16)
    _neighbourhood_attention(u_ref, kv_ref, ctx_ref, bias_ref, y_ref)
    _pool_mixer(u_ref, pp_ref, wpool_ref, pscale_ref, y_ref, DEC_SEQ)
    _conv_mixer(u_ref, hp_ref, sh_ref, wdw_ref, bdw_ref, cng_ref, cnb_ref, y_ref, DEC_SEQ)


def _small_weight_specs():
    return [_layer_spec((POOL_WIDTH, POOL_WIDTH)), _layer_spec((1, POOL_WIDTH)),
            _layer_spec((CONV_K, CONV_WIDTH)), _layer_spec((1, CONV_WIDTH)),
            _layer_spec((1, CONV_WIDTH)), _layer_spec((1, CONV_WIDTH))]


def _mix_scratch(n_tok):
    return [
        pltpu.VMEM((n_tok, N_MIX), F32),
        pltpu.VMEM((n_tok, 2 * NA_WIDTH), BF16),
    ], [
        pltpu.VMEM((n_tok + 2 * POOL_PAD, POOL_WIDTH), F32),
        pltpu.VMEM((n_tok + 2 * CONV_PAD, CONV_WIDTH), F32),
        pltpu.VMEM((SUBLANES, MIX_CHUNK + 2 * CONV_PAD, CONV_WIDTH), F32),
    ]


def _mix_prompt(layer, x_all, mods, g1, w_in, small, cast_w, kbuf, vbuf):
    head, tail = _mix_scratch(SEQ)
    operands = (layer, x_all, mods, g1, w_in, *small, *cast_w, kbuf, vbuf)
    n_in = len(operands)
    chunk_specs, bf_specs, bf_shapes = [], [], []
    for rows, cols in CAST_WEIGHTS:
        r = rows // BATCH
        chunk_specs.append(pl.BlockSpec((None, r, cols), lambda b, l: (l[0], b, 0)))
        out_cols = N_GATE if cols == N_IN else cols
        bf_specs.append(pl.BlockSpec((r, out_cols), lambda b, l: (b, 0)))
        bf_shapes.append(jax.ShapeDtypeStruct((rows, out_cols), BF16))
    grid_spec = pltpu.PrefetchScalarGridSpec(
        num_scalar_prefetch=1,
        grid=(BATCH,),
        in_specs=[pl.BlockSpec((SEQ, D_MODEL), lambda b, l: (b, 0)),
                  _layer_spec((MOD_ROWS, N_MOD * D_MODEL)),
                  _layer_spec((1, D_MODEL)),
                  _layer_spec((D_MODEL, N_MIX))] + _small_weight_specs() + chunk_specs
        + [pl.BlockSpec(memory_space=pl.ANY), pl.BlockSpec(memory_space=pl.ANY)],
        out_specs=[pl.BlockSpec((SEQ, D_MODEL), lambda b, l: (b, 0)),
                   pl.BlockSpec((None, None, SEQ, NA_WIDTH), lambda b, l: (b, l[0], 0, 0)),
                   pl.BlockSpec((None, None, SEQ, NA_WIDTH), lambda b, l: (b, l[0], 0, 0)),
                   pl.BlockSpec((D_MODEL, N_MIX), lambda b, l: (0, 0))] + bf_specs,
        scratch_shapes=head + tail,
    )
    return pl.pallas_call(
        _mix_prompt_kernel,
        grid_spec=grid_spec,
        out_shape=[jax.ShapeDtypeStruct((N_PROMPT_TOK, D_MODEL), BF16),
                   jax.ShapeDtypeStruct(kbuf.shape, F32),
                   jax.ShapeDtypeStruct(vbuf.shape, F32),
                   jax.ShapeDtypeStruct((D_MODEL, N_MIX), BF16)] + bf_shapes,
        input_output_aliases={n_in - 2: 1, n_in - 1: 2},
        compiler_params=pltpu.CompilerParams(
            dimension_semantics=("arbitrary",), vmem_limit_bytes=VMEM_LIMIT),
        name="mix_prompt",
    )(*operands)


def _mix_sample(layer, x_all, mods, g1, wmix, small, cache_k, cache_v, bias):
    head, tail = _mix_scratch(DEC_SEQ)
    first_blk = N_PROMPT_TOK // DEC_SEQ
    ctx_spec = pl.BlockSpec((None, None, PAST_LEN, NA_WIDTH), lambda b, l: (b, l[0], 0, 0))
    grid_spec = pltpu.PrefetchScalarGridSpec(
        num_scalar_prefetch=1,
        grid=(DEC_BATCH,),
        in_specs=[pl.BlockSpec((DEC_SEQ, D_MODEL), lambda b, l: (first_blk + b, 0)),
                  _layer_spec((MOD_ROWS, N_MOD * D_MODEL)),
                  _layer_spec((1, D_MODEL)),
                  _whole_spec((D_MODEL, N_MIX))] + _small_weight_specs()
        + [ctx_spec, ctx_spec, _layer_spec((NA_HEADS, 2, GRID_W, BIAS_TILES * GRID_W))],
        out_specs=pl.BlockSpec((DEC_SEQ, D_MODEL), lambda b, l: (b, 0)),
        scratch_shapes=head + [pltpu.VMEM((PAST_LEN, 2 * NA_WIDTH), BF16)] + tail,
    )
    return pl.pallas_call(
        _mix_sample_kernel,
        grid_spec=grid_spec,
        out_shape=jax.ShapeDtypeStruct((N_SAMPLE_TOK, D_MODEL), BF16),
        compiler_params=pltpu.CompilerParams(
            dimension_semantics=("arbitrary",), vmem_limit_bytes=VMEM_LIMIT),
        name="mix_sample",
    )(layer, x_all, mods, g1, wmix, *small, cache_k, cache_v, bias)


FFN_TILE = 2 * TOK_TILE
PROMPT_TILES = N_PROMPT_TOK // FFN_TILE
SAMPLE_TILES = N_SAMPLE_TOK // FFN_TILE


def _ffn_kernel(final, first_tile, layer_ref, x_ref, yp_ref, ys_ref, mod_ref, g1_ref, g2_ref, fg_ref,
                bg_ref, wg_ref, woa_ref, wob_ref, woc_ref, wout_ref, wgu_ref, wdown_ref, o_ref):
    del layer_ref
    i = pl.program_id(0) + first_tile
    is_prompt = i < PROMPT_TILES
    row = jnp.where(is_prompt, 0, 1 + (i - PROMPT_TILES) // (DEC_SEQ // FFN_TILE))

    def mod(j):
        return mod_ref[pl.ds(row, 1), j * D_MODEL:(j + 1) * D_MODEL]

    branches = ((0, NA_WIDTH, woa_ref), (NA_WIDTH, NA_WIDTH + POOL_WIDTH, wob_ref),
                (NA_WIDTH + POOL_WIDTH, D_MODEL, woc_ref))
    halves = [slice(t0, t0 + TOK_TILE) for t0 in range(0, FFN_TILE, TOK_TILE)]
    xs = [x_ref[rows, :] for rows in halves]
    hs = [((_rms(x) * g1_ref[...]) * (1.0 + mod(1)) + mod(0)).astype(BF16) for x in xs]
    merged = [None] * len(halves)
    for br, (c0, c1, w_ref) in enumerate(branches):
        g0 = br * D_MODEL
        gates = [_sigmoid(_dot(h, wg_ref[:, g0:g0 + D_MODEL]) + bg_ref[:, g0:g0 + D_MODEL])
                 for h in hs]
        ys = [jnp.where(is_prompt, yp_ref[rows, c0:c1], ys_ref[rows, c0:c1]) for rows in halves]
        terms = [gate * _dot(y, w_ref[...]) for gate, y in zip(gates, ys)]
        merged = [t if m is None else m + t for m, t in zip(merged, terms)]
    x1s = [x + mod(2) * _dot(m.astype(BF16), wout_ref[...]) for x, m in zip(xs, merged)]
    h2s = [((_rms(x1) * g2_ref[...]) * (1.0 + mod(4)) + mod(3)).astype(BF16) for x1 in x1s]
    accs = [None] * len(halves)
    for c0, c1 in FFN_CHUNKS:
        a_s = [_dot(h2, wgu_ref[:, c0:c1]) for h2 in h2s]
        b_s = [_dot(h2, wgu_ref[:, FFN_HIDDEN + c0:FFN_HIDDEN + c1]) for h2 in h2s]
        ts = [((a * _sigmoid(a)) * b).astype(BF16) for a, b in zip(a_s, b_s)]
        ds = [_dot(t, wdown_ref[c0:c1, :]) for t in ts]
        accs = [d if acc is None else acc + d for acc, d in zip(accs, ds)]
    for rows, x1, acc in zip(halves, x1s, accs):
        x2 = x1 + mod(5) * acc
        if final:
            x2 = _rms(x2) * fg_ref[...]
        o_ref[rows, :] = x2


def _ffn(layer, x_all, y_p, y_s, mods, g1, g2, fg, b_gate, weights, final,
         first_tile=0, n_tiles=N_TOK // FFN_TILE):
    f = first_tile
    x_spec = pl.BlockSpec((FFN_TILE, D_MODEL), lambda i, l: (i + f, 0))
    yp_spec = pl.BlockSpec((FFN_TILE, D_MODEL),
                           lambda i, l: (jnp.minimum(i + f, PROMPT_TILES - 1), 0))
    ys_spec = pl.BlockSpec((FFN_TILE, D_MODEL),
                           lambda i, l: (jnp.maximum(i + f - PROMPT_TILES, 0), 0))
    grid_spec = pltpu.PrefetchScalarGridSpec(
        num_scalar_prefetch=1,
        grid=(n_tiles,),
        in_specs=[x_spec, yp_spec, ys_spec,
                  _layer_spec((MOD_ROWS, N_MOD * D_MODEL)),
                  _layer_spec((1, D_MODEL)), _layer_spec((1, D_MODEL)), _whole_spec((1, D_MODEL)),
                  _layer_spec((1, N_GATE)),
                  _whole_spec((D_MODEL, N_GATE)),
                  _whole_spec((NA_WIDTH, D_MODEL)), _whole_spec((POOL_WIDTH, D_MODEL)),
                  _whole_spec((CONV_WIDTH, D_MODEL)), _whole_spec((D_MODEL, D_MODEL)),
                  _whole_spec((D_MODEL, 2 * FFN_HIDDEN)), _whole_spec((FFN_HIDDEN, D_MODEL))],
        out_specs=pl.BlockSpec((FFN_TILE, D_MODEL), lambda i, l: (i, 0)),
    )
    return pl.pallas_call(
        functools.partial(_ffn_kernel, final, first_tile),
        grid_spec=grid_spec,
        out_shape=jax.ShapeDtypeStruct((n_tiles * FFN_TILE, D_MODEL), F32),
        compiler_params=pltpu.CompilerParams(
            dimension_semantics=("arbitrary",), vmem_limit_bytes=VMEM_LIMIT),
        name="merge_ffn",
    )(layer, x_all, y_p, y_s, mods, g1, g2, fg, b_gate, *weights)


def _block_diag(w_pool):
    out = jnp.zeros((DEPTH, POOL_WIDTH, POOL_WIDTH), w_pool.dtype)
    for g in range(POOL_GROUPS):
        a = g * POOL_GROUP_DIM
        out = out.at[:, a:a + POOL_GROUP_DIM, a:a + POOL_GROUP_DIM].set(w_pool[:, g])
    return out


def kernel(x_prompt, x_sample, cache_k, cache_v, c, c_ctx, w_mod, b_mod, norm1_g, norm2_g, w_in,
           b_gate, rpb, w_oa, w_pool, pool_scale, w_ob, w_dw, b_dw, conv_norm_g, conv_norm_b,
           w_oc, w_out, w_gu, w_down, final_g):
    x_all = jnp.concatenate([x_prompt.reshape(N_PROMPT_TOK, D_MODEL),
                             x_sample.reshape(N_SAMPLE_TOK, D_MODEL)], axis=0)
    cond = jnp.concatenate([c_ctx[None, :], c,
                            jnp.zeros((MOD_ROWS - 1 - DEC_BATCH, D_MODEL), F32)], axis=0)
    mods = _modulation(cond, w_mod, b_mod)
    bias = _bias_tables(rpb)
    ck = cache_k.reshape(DEC_BATCH, DEPTH, PAST_LEN, NA_WIDTH)
    cv = cache_v.reshape(DEC_BATCH, DEPTH, PAST_LEN, NA_WIDTH)
    g1 = norm1_g.reshape(DEPTH, 1, D_MODEL)
    g2 = norm2_g.reshape(DEPTH, 1, D_MODEL)
    fg = final_g.reshape(1, D_MODEL)
    bg = b_gate.reshape(DEPTH, 1, N_GATE)
    small = (_block_diag(w_pool), pool_scale.reshape(DEPTH, 1, POOL_WIDTH), w_dw,
             b_dw.reshape(DEPTH, 1, CONV_WIDTH), conv_norm_g.reshape(DEPTH, 1, CONV_WIDTH),
             conv_norm_b.reshape(DEPTH, 1, CONV_WIDTH))
    cast_w = (w_in, w_oa, w_ob, w_oc, w_out, w_gu, w_down)
    kbuf = jnp.zeros((BATCH, DEPTH, SEQ, NA_WIDTH), F32)
    vbuf = jnp.zeros((BATCH, DEPTH, SEQ, NA_WIDTH), F32)
    for l in range(DEPTH):
        layer = jnp.full((1,), l, jnp.int32)
        y_p, kbuf, vbuf, wmix, *ffn_w = _mix_prompt(layer, x_all, mods, g1, w_in, small, cast_w,
                                                    kbuf, vbuf)
        y_s = _mix_sample(layer, x_all, mods, g1, wmix, small, ck, cv, bias)
        ffn_args = (layer, x_all, y_p, y_s, mods, g1, g2, fg, bg, ffn_w)
        if l < DEPTH - 1:
            x_all = _ffn(*ffn_args, False)
    y_prompt = _ffn(*ffn_args, True, 0, PROMPT_TILES).reshape(BATCH, SEQ, D_MODEL)
    y_sample = _ffn(*ffn_args, True, PROMPT_TILES, SAMPLE_TILES).reshape(
        DEC_BATCH, DEC_SEQ, D_MODEL)
    kv_shape = (BATCH, DEPTH, SEQ, NA_HEADS, NA_HEAD_DIM)
    new_k = kbuf.reshape(kv_shape)
    new_v = vbuf.reshape(kv_shape)
    return (y_prompt, y_sample, new_k, new_v)
```

```python
import functools

import jax
import jax.numpy as jnp
from jax import lax
from jax.experimental import pallas as pl
from jax.experimental.pallas import tpu as pltpu

D_MODEL = 1024
BATCH = 16
SEQ = 256
DEPTH = 4
DEC_BATCH = 2
DEC_SEQ = 1024
PAST_LEN = 256
GRID_W = 64
GRID_ROWS = DEC_SEQ // GRID_W
NA_HEADS = 8
NA_HEAD_DIM = 64
NA_WIDTH = NA_HEADS * NA_HEAD_DIM
NA_KH = 8
NA_KW = 16
POOL_WIDTH = 256
POOL_GROUPS = 4
POOL_GROUP_DIM = 64
CONV_WIDTH = 256
CONV_K = 31
N_BRANCH = 3
N_MIX = 3 * NA_WIDTH + POOL_WIDTH + 2 * CONV_WIDTH
N_GATE = N_BRANCH * D_MODEL
N_IN = N_MIX + N_GATE
FFN_HIDDEN = 2816
N_MOD = 6
EPS = 1e-6
NEG_INF = -1e30

N_PROMPT_TOK = BATCH * SEQ
N_SAMPLE_TOK = DEC_BATCH * DEC_SEQ
N_TOK = N_PROMPT_TOK + N_SAMPLE_TOK

LANES = 128
SUBLANES = 8
MOD_ROWS = 8
MOD_BLOCK_N = 1536
TOK_TILE = 256
ATT_BLOCK_ROWS = 4
BIAS_TILES = 16
POOL_PAD = 8
CONV_PAD = 16
MIX_CHUNK = 128
FFN_CHUNKS = ((0, 1024), (1024, 2048), (2048, 2816))
VMEM_LIMIT = 56 * 1024 * 1024

F32 = jnp.float32
BF16 = jnp.bfloat16

CAST_WEIGHTS = ((D_MODEL, N_IN), (NA_WIDTH, D_MODEL), (POOL_WIDTH, D_MODEL), (CONV_WIDTH, D_MODEL),
                (D_MODEL, D_MODEL), (D_MODEL, 2 * FFN_HIDDEN), (FFN_HIDDEN, D_MODEL))


def _sigmoid(x):
    return 1.0 / (1.0 + jnp.exp(-x))


def _rms(x):
    return x * lax.rsqrt(jnp.mean(x * x, axis=-1, keepdims=True) + EPS)


def _dot(a, b):
    return jnp.dot(a, b, preferred_element_type=F32)


def _dot_nt(a, b):
    return lax.dot_general(a, b, (((1,), (1,)), ((), ())), preferred_element_type=F32)


def _layer_spec(shape):
    zeros = (0,) * len(shape)
    return pl.BlockSpec((None,) + tuple(shape), lambda i, l: (l[0],) + zeros,
                        pipeline_mode=pl.Buffered(1))


def _whole_spec(shape):
    zeros = (0,) * len(shape)
    return pl.BlockSpec(tuple(shape), lambda i, l: zeros, pipeline_mode=pl.Buffered(1))


def _mod_kernel(cond_ref, w_ref, b_ref, out_ref):
    c = cond_ref[...]
    s = (c * _sigmoid(c)).astype(BF16)
    out_ref[...] = _dot(s, w_ref[...].astype(BF16)) + b_ref[...]


def _modulation(cond, w_mod, b_mod):
    n = N_MOD * D_MODEL
    return pl.pallas_call(
        _mod_kernel,
        grid=(DEPTH, n // MOD_BLOCK_N),
        in_specs=[
            pl.BlockSpec((MOD_ROWS, D_MODEL), lambda l, j: (0, 0)),
            pl.BlockSpec((None, D_MODEL, MOD_BLOCK_N), lambda l, j: (l, 0, j)),
            pl.BlockSpec((None, 1, MOD_BLOCK_N), lambda l, j: (l, 0, j)),
        ],
        out_specs=pl.BlockSpec((None, MOD_ROWS, MOD_BLOCK_N), lambda l, j: (l, 0, j)),
        out_shape=jax.ShapeDtypeStruct((DEPTH, MOD_ROWS, n), F32),
        compiler_params=pltpu.CompilerParams(
            dimension_semantics=("arbitrary", "arbitrary"), vmem_limit_bytes=VMEM_LIMIT),
        name="modulation",
    )(cond, w_mod, b_mod.reshape(DEPTH, 1, n))


def _bias_kernel(rpb_ref, out_ref):
    cq = lax.broadcasted_iota(jnp.int32, (GRID_W, LANES), 0)
    ln = lax.broadcasted_iota(jnp.int32, (GRID_W, LANES), 1)
    ck = jnp.where(ln < GRID_W, ln, ln - GRID_W)
    cs = jnp.clip(cq - NA_KW // 2, 0, GRID_W - NA_KW)
    col_in = (ck >= cs) & (ck < cs + NA_KW)
    low_half = ln < GRID_W
    neg = jnp.full((GRID_W, LANES), NEG_INF, F32)
    lo = GRID_W - NA_KW

    for head in range(NA_HEADS):
        rp = rpb_ref[head]
        lane = lax.broadcasted_iota(jnp.int32, rp.shape, 1)
        rolled = pltpu.roll(rp, lo, axis=1)
        first = rp[:, 0:1]
        last = rp[:, 2 * NA_KW - 2:2 * NA_KW - 1]
        ext = jnp.where(lane < lo, first, jnp.where(lane > lo + 2 * NA_KW - 2, last, rolled))

        def toeplitz(tile, shift):
            dr = tile - (NA_KH - 1)
            if dr < -(NA_KH - 1) or dr > NA_KH - 1:
                return neg
            row = jnp.broadcast_to(ext[dr + NA_KH - 1:dr + NA_KH, :], (GRID_W, LANES))
            return pltpu.roll(row, shift, axis=1, stride=1, stride_axis=0)

        for copy in range(2):
            for col in range(BIAS_TILES // 2):
                t_lo = 2 * col + copy
                t_hi = t_lo + 1
                a = toeplitz(t_lo, GRID_W + 1)
                b = toeplitz(t_hi, 1)
                pair = jnp.where(low_half, a, b)
                pair = jnp.where(col_in, pair, NEG_INF)
                out_ref[head, copy, :, col * LANES:(col + 1) * LANES] = pair


def _bias_tables(rpb):
    rp = jnp.pad(rpb, ((0, 0), (0, 0), (0, 1), (0, LANES - (2 * NA_KW - 1))))
    width = BIAS_TILES * GRID_W
    return pl.pallas_call(
        _bias_kernel,
        grid=(DEPTH,),
        in_specs=[pl.BlockSpec((None, NA_HEADS, 2 * NA_KH, LANES), lambda l: (l, 0, 0, 0))],
        out_specs=pl.BlockSpec((None, NA_HEADS, 2, GRID_W, width), lambda l: (l, 0, 0, 0, 0)),
        out_shape=jax.ShapeDtypeStruct((DEPTH, NA_HEADS, 2, GRID_W, width), F32),
        compiler_params=pltpu.CompilerParams(
            dimension_semantics=("arbitrary",), vmem_limit_bytes=VMEM_LIMIT),
        name="bias_tables",
    )(rp)


def _in_proj(x_ref, mod_ref, row, g1_ref, win_ref, u_ref, n_tok):
    shift1 = mod_ref[pl.ds(row, 1), 0:D_MODEL]
    scale1 = mod_ref[pl.ds(row, 1), D_MODEL:2 * D_MODEL]
    for t0 in range(0, n_tok, TOK_TILE):
        x = x_ref[t0:t0 + TOK_TILE, :]
        h = ((_rms(x) * g1_ref[...]) * (1.0 + scale1) + shift1).astype(BF16)
        u_ref[t0:t0 + TOK_TILE, :] = _dot(h, win_ref[...])


def _softmax_pv(parts):
    m = None
    for s, _ in parts:
        mi = jnp.max(s, axis=-1, keepdims=True)
        m = mi if m is None else jnp.maximum(m, mi)
    den = None
    acc = None
    for s, v in parts:
        e = jnp.exp(s - m)
        li = jnp.sum(e, axis=-1, keepdims=True)
        oi = _dot(e.astype(BF16), v)
        den = li if den is None else den + li
        acc = oi if acc is None else acc + oi
    return acc / den


def _context_attention(u_ref, kv_ref, y_ref, n_tok):
    scale = NA_HEAD_DIM ** -0.5
    lane = lax.broadcasted_iota(jnp.int32, (n_tok, LANES), 1)
    low = lane < NA_HEAD_DIM
    for hp in range(NA_HEADS // 2):
        c0 = hp * LANES
        qpair = u_ref[:, c0:c0 + LANES] * scale
        kpair = kv_ref[:, c0:c0 + LANES]
        vpair = kv_ref[:, NA_WIDTH + c0:NA_WIDTH + c0 + LANES]
        outs = []
        for hh in range(2):
            q = jnp.where(low if hh == 0 else ~low, qpair, 0.0).astype(BF16)
            outs.append(_softmax_pv([(_dot_nt(q, kpair), vpair)]))
        y_ref[:, c0:c0 + LANES] = jnp.where(low, outs[0], outs[1]).astype(BF16)


def _window_start(r):
    return min(max(r - NA_KH // 2, 0), GRID_ROWS - NA_KH)


def _neighbourhood_attention(u_ref, kv_ref, ctx_ref, bias_ref, y_ref):
    scale = NA_HEAD_DIM ** -0.5
    blk_tok = ATT_BLOCK_ROWS * GRID_W
    lane = lax.broadcasted_iota(jnp.int32, (blk_tok, LANES), 1)
    low = lane < NA_HEAD_DIM
    for blk in range(GRID_ROWS // ATT_BLOCK_ROWS):
        r0 = blk * ATT_BLOCK_ROWS
        first_row = _window_start(r0)
        key_rows = _window_start(r0 + ATT_BLOCK_ROWS - 1) + NA_KH - first_row
        key_rows += key_rows % 2
        j0 = min(first_row, GRID_ROWS - key_rows)
        key_tok = key_rows * GRID_W
        klane = lax.broadcasted_iota(jnp.int32, (GRID_W, key_tok), 1)
        q0 = r0 * GRID_W
        k0 = j0 * GRID_W
        for hp in range(NA_HEADS // 2):
            c0 = hp * LANES
            qpair = u_ref[q0:q0 + blk_tok, c0:c0 + LANES] * scale
            kpair = kv_ref[k0:k0 + key_tok, c0:c0 + LANES]
            vpair = kv_ref[k0:k0 + key_tok, NA_WIDTH + c0:NA_WIDTH + c0 + LANES]
            kctx = ctx_ref[:, c0:c0 + LANES]
            vctx = ctx_ref[:, NA_WIDTH + c0:NA_WIDTH + c0 + LANES]
            outs = []
            for hh in range(2):
                head = 2 * hp + hh
                q = jnp.where(low if hh == 0 else ~low, qpair, 0.0).astype(BF16)
                s_loc = _dot_nt(q, kpair)
                s_ctx = _dot_nt(q, kctx)
                strips = []
                for rr in range(ATT_BLOCK_ROWS):
                    r = r0 + rr
                    rs = _window_start(r)
                    tile0 = NA_KH - 1 + j0 - r
                    assert 0 <= tile0 and tile0 + key_rows <= BIAS_TILES
                    copy = tile0 % 2
                    b0 = (tile0 - copy) * GRID_W
                    bias = bias_ref[head, copy, :, b0:b0 + key_tok]
                    s = s_loc[rr * GRID_W:(rr + 1) * GRID_W, :] + bias
                    if key_rows > NA_KH:
                        valid = ((klane >= (rs - j0) * GRID_W)
                                 & (klane < (rs - j0 + NA_KH) * GRID_W))
                        s = jnp.where(valid, s, NEG_INF)
                    else:
                        assert rs == j0
                    strips.append(s)
                s_loc = jnp.concatenate(strips, axis=0)
                outs.append(_softmax_pv([(s_loc, vpair), (s_ctx, vctx)]))
            y_ref[q0:q0 + blk_tok, c0:c0 + LANES] = jnp.where(low, outs[0], outs[1]).astype(BF16)


def _pool_mixer(u_ref, pp_ref, wpool_ref, pscale_ref, y_ref, n_tok):
    c_in = 3 * NA_WIDTH
    zeros = jnp.zeros((POOL_PAD, POOL_WIDTH), F32)
    pp_ref[0:POOL_PAD, :] = zeros
    pp_ref[POOL_PAD + n_tok:2 * POOL_PAD + n_tok, :] = zeros
    pp_ref[POOL_PAD:POOL_PAD + n_tok, :] = u_ref[:, c_in:c_in + POOL_WIDTH]
    lane = lax.broadcasted_iota(jnp.int32, (MIX_CHUNK, LANES), 1)
    first = lane < POOL_GROUP_DIM
    tok = lax.broadcasted_iota(jnp.int32, (MIX_CHUNK, LANES), 0)
    wpool = wpool_ref[...].astype(BF16)

    for t0 in range(0, n_tok, MIX_CHUNK):
        def ld(d, half):
            a = POOL_PAD + t0 + d
            return pp_ref[a:a + MIX_CHUNK, half * LANES:(half + 1) * LANES]

        def window(half, lo, hi):
            acc = None
            for d in range(lo, hi):
                acc = ld(d, half) if acc is None else acc + ld(d, half)
            return acc

        a2 = window(0, -1, 1)
        a4 = a2 + ld(-2, 0) + ld(1, 0)
        a8 = window(1, -4, 4)
        a16 = a8 + window(1, -8, -4) + window(1, 4, 8)
        t = tok + t0
        halves = []
        for half, (small, big, ws, wb) in enumerate(((a2, a4, 1, 2), (a8, a16, 4, 8))):
            hw = jnp.where(first, ws, wb)
            cnt = (jnp.minimum(t + hw, n_tok) - jnp.maximum(t - hw, 0)).astype(F32)
            halves.append(jnp.where(first, small, big) / cnt - ld(0, half))
        pooled = jnp.concatenate(halves, axis=1).astype(BF16)
        y = _dot(pooled, wpool) * pscale_ref[...]
        y_ref[t0:t0 + MIX_CHUNK, NA_WIDTH:NA_WIDTH + POOL_WIDTH] = y.astype(BF16)


def _conv_mixer(u_ref, hp_ref, sh_ref, wdw_ref, bdw_ref, cng_ref, cnb_ref, y_ref, n_tok):
    c_a = 3 * NA_WIDTH + POOL_WIDTH
    c_g = c_a + CONV_WIDTH
    zeros = jnp.zeros((CONV_PAD, CONV_WIDTH), F32)
    hp_ref[0:CONV_PAD, :] = zeros
    hp_ref[CONV_PAD + n_tok:2 * CONV_PAD + n_tok, :] = zeros
    for t0 in range(0, n_tok, TOK_TILE):
        a = u_ref[t0:t0 + TOK_TILE, c_a:c_a + CONV_WIDTH]
        g = u_ref[t0:t0 + TOK_TILE, c_g:c_g + CONV_WIDTH]
        hp_ref[CONV_PAD + t0:CONV_PAD + t0 + TOK_TILE, :] = a * _sigmoid(g)
    off = CONV_PAD - CONV_K // 2
    span = MIX_CHUNK + (off + CONV_K - 1) // SUBLANES * SUBLANES
    for t0 in range(0, n_tok, MIX_CHUNK):
        for s in range(1, SUBLANES):
            sh_ref[s, 0:span, :] = hp_ref[t0 + s:t0 + s + span, :]
        acc = None
        for k in range(CONV_K):
            a, s = divmod(off + k, SUBLANES)
            if s == 0:
                win = hp_ref[t0 + a * SUBLANES:t0 + a * SUBLANES + MIX_CHUNK, :]
            else:
                win = sh_ref[s, a * SUBLANES:a * SUBLANES + MIX_CHUNK, :]
            term = wdw_ref[k:k + 1, :] * win
            acc = term if acc is None else acc + term
        acc = acc + bdw_ref[...]
        xc = acc - jnp.mean(acc, axis=-1, keepdims=True)
        y = xc * lax.rsqrt(jnp.mean(xc * xc, axis=-1, keepdims=True) + EPS)
        y = y * cng_ref[...] + cnb_ref[...]
        y_ref[t0:t0 + MIX_CHUNK, NA_WIDTH + POOL_WIDTH:D_MODEL] = (y * _sigmoid(y)).astype(BF16)


def _mix_prompt_kernel(first_layer, layer_ref, x_ref, mod_ref, g1_ref, win_ref, wpool_ref, pscale_ref,
                       wdw_ref, bdw_ref, cng_ref, cnb_ref,
                       cin_ref, coa_ref, cob_ref, coc_ref, cout_ref, cgu_ref, cdown_ref, *rest):
    del layer_ref
    if not first_layer:
        rest = rest[2:]
    (y_ref, k_ref, v_ref, wmix_ref, bg_ref, boa_ref, bob_ref, boc_ref, bout_ref, bgu_ref, bdown_ref,
     u_ref, kv_ref, pp_ref, hp_ref, sh_ref) = rest
    if first_layer:
        for ref in (k_ref, v_ref):
            ref[1:DEPTH] = jnp.zeros((DEPTH - 1, SEQ, NA_WIDTH), F32)
        k_ref, v_ref = k_ref.at[0], v_ref.at[0]

    @pl.when(pl.program_id(0) == 0)
    def _():
        wmix_ref[...] = win_ref[...].astype(BF16)

    bg_ref[...] = cin_ref[:, N_MIX:].astype(BF16)
    for src, dst in ((coa_ref, boa_ref), (cob_ref, bob_ref), (coc_ref, boc_ref),
                     (cout_ref, bout_ref), (cgu_ref, bgu_ref), (cdown_ref, bdown_ref)):
        dst[...] = src[...].astype(BF16)

    _in_proj(x_ref, mod_ref, 0, g1_ref, wmix_ref, u_ref, SEQ)
    k_ref[...] = u_ref[:, NA_WIDTH:2 * NA_WIDTH]
    v_ref[...] = u_ref[:, 2 * NA_WIDTH:3 * NA_WIDTH]
    kv_ref[...] = u_ref[:, NA_WIDTH:3 * NA_WIDTH].astype(BF16)
    _context_attention(u_ref, kv_ref, y_ref, SEQ)
    _pool_mixer(u_ref, pp_ref, wpool_ref, pscale_ref, y_ref, SEQ)
    _conv_mixer(u_ref, hp_ref, sh_ref, wdw_ref, bdw_ref, cng_ref, cnb_ref, y_ref, SEQ)


def _mix_sample_kernel(layer_ref, x_ref, mod_ref, g1_ref, wmix_ref, wpool_ref, pscale_ref, wdw_ref,
                       bdw_ref, cng_ref, cnb_ref, ck_ref, cv_ref, bias_ref, y_ref,
                       u_ref, kv_ref, ctx_ref, pp_ref, hp_ref, sh_ref):
    del layer_ref
    row = 1 + pl.program_id(0)
    _in_proj(x_ref, mod_ref, row, g1_ref, wmix_ref, u_ref, DEC_SEQ)
    kv_ref[...] = u_ref[:, NA_WIDTH:3 * NA_WIDTH].astype(BF16)
    ctx_ref[:, 0:NA_WIDTH] = ck_ref[...].astype(BF16)
    ctx_ref[:, NA_WIDTH:2 * NA_WIDTH] = cv_ref[...].astype(BF16)
    _neighbourhood_attention(u_ref, kv_ref, ctx_ref, bias_ref, y_ref)
    _pool_mixer(u_ref, pp_ref, wpool_ref, pscale_ref, y_ref, DEC_SEQ)
    _conv_mixer(u_ref, hp_ref, sh_ref, wdw_ref, bdw_ref, cng_ref, cnb_ref, y_ref, DEC_SEQ)


def _small_weight_specs():
    return [_layer_spec((POOL_WIDTH, POOL_WIDTH)), _layer_spec((1, POOL_WIDTH)),
            _layer_spec((CONV_K, CONV_WIDTH)), _layer_spec((1, CONV_WIDTH)),
            _layer_spec((1, CONV_WIDTH)), _layer_spec((1, CONV_WIDTH))]


def _mix_scratch(n_tok):
    return [
        pltpu.VMEM((n_tok, N_MIX), F32),
        pltpu.VMEM((n_tok, 2 * NA_WIDTH), BF16),
    ], [
        pltpu.VMEM((n_tok + 2 * POOL_PAD, POOL_WIDTH), F32),
        pltpu.VMEM((n_tok + 2 * CONV_PAD, CONV_WIDTH), F32),
        pltpu.VMEM((SUBLANES, MIX_CHUNK + 2 * CONV_PAD, CONV_WIDTH), F32),
    ]


def _mix_prompt(layer, x_src, mods, g1, w_in, small, cast_w, kbuf=None, vbuf=None):
    head, tail = _mix_scratch(SEQ)
    first_layer = kbuf is None
    operands = (layer, x_src, mods, g1, w_in, *small, *cast_w)
    kv_shape = (BATCH, DEPTH, SEQ, NA_WIDTH)
    if first_layer:
        kv_in_specs, aliases = [], {}
        kv_spec = pl.BlockSpec((None, DEPTH, SEQ, NA_WIDTH), lambda b, l: (b, 0, 0, 0))
    else:
        operands += (kbuf, vbuf)
        kv_in_specs = [pl.BlockSpec(memory_space=pl.ANY), pl.BlockSpec(memory_space=pl.ANY)]
        aliases = {len(operands) - 2: 1, len(operands) - 1: 2}
        kv_spec = pl.BlockSpec((None, None, SEQ, NA_WIDTH), lambda b, l: (b, l[0], 0, 0))
    chunk_specs, bf_specs, bf_shapes = [], [], []
    for rows, cols in CAST_WEIGHTS:
        r = rows // BATCH
        chunk_specs.append(pl.BlockSpec((None, r, cols), lambda b, l: (l[0], b, 0)))
        out_cols = N_GATE if cols == N_IN else cols
        bf_specs.append(pl.BlockSpec((r, out_cols), lambda b, l: (b, 0)))
        bf_shapes.append(jax.ShapeDtypeStruct((rows, out_cols), BF16))
    grid_spec = pltpu.PrefetchScalarGridSpec(
        num_scalar_prefetch=1,
        grid=(BATCH,),
        in_specs=[pl.BlockSpec((SEQ, D_MODEL), lambda b, l: (b, 0)),
                  _layer_spec((MOD_ROWS, N_MOD * D_MODEL)),
                  _layer_spec((1, D_MODEL)),
                  _layer_spec((D_MODEL, N_MIX))] + _small_weight_specs() + chunk_specs
        + kv_in_specs,
        out_specs=[pl.BlockSpec((SEQ, D_MODEL), lambda b, l: (b, 0)), kv_spec, kv_spec,
                   pl.BlockSpec((D_MODEL, N_MIX), lambda b, l: (0, 0))] + bf_specs,
        scratch_shapes=head + tail,
    )
    return pl.pallas_call(
        functools.partial(_mix_prompt_kernel, first_layer),
        grid_spec=grid_spec,
        out_shape=[jax.ShapeDtypeStruct((N_PROMPT_TOK, D_MODEL), BF16),
                   jax.ShapeDtypeStruct(kv_shape, F32),
                   jax.ShapeDtypeStruct(kv_shape, F32),
                   jax.ShapeDtypeStruct((D_MODEL, N_MIX), BF16)] + bf_shapes,
        input_output_aliases=aliases,
        compiler_params=pltpu.CompilerParams(
            dimension_semantics=("arbitrary",), vmem_limit_bytes=VMEM_LIMIT),
        name="mix_prompt",
    )(*operands)


def _mix_sample(layer, x_src, first_blk, mods, g1, wmix, small, cache_k, cache_v, bias):
    head, tail = _mix_scratch(DEC_SEQ)
    ctx_spec = pl.BlockSpec((None, None, PAST_LEN, NA_WIDTH), lambda b, l: (b, l[0], 0, 0))
    grid_spec = pltpu.PrefetchScalarGridSpec(
        num_scalar_prefetch=1,
        grid=(DEC_BATCH,),
        in_specs=[pl.BlockSpec((DEC_SEQ, D_MODEL), lambda b, l: (first_blk + b, 0)),
                  _layer_spec((MOD_ROWS, N_MOD * D_MODEL)),
                  _layer_spec((1, D_MODEL)),
                  _whole_spec((D_MODEL, N_MIX))] + _small_weight_specs()
        + [ctx_spec, ctx_spec, _layer_spec((NA_HEADS, 2, GRID_W, BIAS_TILES * GRID_W))],
        out_specs=pl.BlockSpec((DEC_SEQ, D_MODEL), lambda b, l: (b, 0)),
        scratch_shapes=head + [pltpu.VMEM((PAST_LEN, 2 * NA_WIDTH), BF16)] + tail,
    )
    return pl.pallas_call(
        _mix_sample_kernel,
        grid_spec=grid_spec,
        out_shape=jax.ShapeDtypeStruct((N_SAMPLE_TOK, D_MODEL), BF16),
        compiler_params=pltpu.CompilerParams(
            dimension_semantics=("arbitrary",), vmem_limit_bytes=VMEM_LIMIT),
        name="mix_sample",
    )(layer, x_src, mods, g1, wmix, *small, cache_k, cache_v, bias)


FFN_TILE = 2 * TOK_TILE
PROMPT_TILES = N_PROMPT_TOK // FFN_TILE
SAMPLE_TILES = N_SAMPLE_TOK // FFN_TILE


def _ffn_kernel(final, first_tile, layer_ref, xp_ref, xs_ref, yp_ref, ys_ref, mod_ref, g1_ref, g2_ref,
                fg_ref, bg_ref, wg_ref, woa_ref, wob_ref, woc_ref, wout_ref, wgu_ref, wdown_ref,
                o_ref):
    del layer_ref
    i = pl.program_id(0) + first_tile
    is_prompt = i < PROMPT_TILES
    row = jnp.where(is_prompt, 0, 1 + (i - PROMPT_TILES) // (DEC_SEQ // FFN_TILE))

    def mod(j):
        return mod_ref[pl.ds(row, 1), j * D_MODEL:(j + 1) * D_MODEL]

    branches = ((0, NA_WIDTH, woa_ref), (NA_WIDTH, NA_WIDTH + POOL_WIDTH, wob_ref),
                (NA_WIDTH + POOL_WIDTH, D_MODEL, woc_ref))
    halves = [slice(t0, t0 + TOK_TILE) for t0 in range(0, FFN_TILE, TOK_TILE)]
    xs = [jnp.where(is_prompt, xp_ref[rows, :], xs_ref[rows, :]) for rows in halves]
    hs = [((_rms(x) * g1_ref[...]) * (1.0 + mod(1)) + mod(0)).astype(BF16) for x in xs]
    merged = [None] * len(halves)
    for br, (c0, c1, w_ref) in enumerate(branches):
        g0 = br * D_MODEL
        gates = [_sigmoid(_dot(h, wg_ref[:, g0:g0 + D_MODEL]) + bg_ref[:, g0:g0 + D_MODEL])
                 for h in hs]
        ys = [jnp.where(is_prompt, yp_ref[rows, c0:c1], ys_ref[rows, c0:c1]) for rows in halves]
        terms = [gate * _dot(y, w_ref[...]) for gate, y in zip(gates, ys)]
        merged = [t if m is None else m + t for m, t in zip(merged, terms)]
    x1s = [x + mod(2) * _dot(m.astype(BF16), wout_ref[...]) for x, m in zip(xs, merged)]
    h2s = [((_rms(x1) * g2_ref[...]) * (1.0 + mod(4)) + mod(3)).astype(BF16) for x1 in x1s]
    accs = [None] * len(halves)
    for c0, c1 in FFN_CHUNKS:
        a_s = [_dot(h2, wgu_ref[:, c0:c1]) for h2 in h2s]
        b_s = [_dot(h2, wgu_ref[:, FFN_HIDDEN + c0:FFN_HIDDEN + c1]) for h2 in h2s]
        ts = [((a * _sigmoid(a)) * b).astype(BF16) for a, b in zip(a_s, b_s)]
        ds = [_dot(t, wdown_ref[c0:c1, :]) for t in ts]
        accs = [d if acc is None else acc + d for acc, d in zip(accs, ds)]
    for rows, x1, acc in zip(halves, x1s, accs):
        x2 = x1 + mod(5) * acc
        if final:
            x2 = _rms(x2) * fg_ref[...]
        o_ref[rows, :] = x2


def _ffn(layer, x_p, x_s, sample_tile0, y_p, y_s, mods, g1, g2, fg, b_gate, weights, final,
         first_tile=0, n_tiles=N_TOK // FFN_TILE):
    f = first_tile
    p_spec = pl.BlockSpec((FFN_TILE, D_MODEL),
                          lambda i, l: (jnp.minimum(i + f, PROMPT_TILES - 1), 0))
    s_spec = pl.BlockSpec((FFN_TILE, D_MODEL),
                          lambda i, l: (jnp.maximum(i + f - PROMPT_TILES, 0), 0))
    xs_spec = pl.BlockSpec((FFN_TILE, D_MODEL),
                           lambda i, l: (sample_tile0 + jnp.maximum(i + f - PROMPT_TILES, 0), 0))
    grid_spec = pltpu.PrefetchScalarGridSpec(
        num_scalar_prefetch=1,
        grid=(n_tiles,),
        in_specs=[p_spec, xs_spec, p_spec, s_spec,
                  _layer_spec((MOD_ROWS, N_MOD * D_MODEL)),
                  _layer_spec((1, D_MODEL)), _layer_spec((1, D_MODEL)), _whole_spec((1, D_MODEL)),
                  _layer_spec((1, N_GATE)),
                  _whole_spec((D_MODEL, N_GATE)),
                  _whole_spec((NA_WIDTH, D_MODEL)), _whole_spec((POOL_WIDTH, D_MODEL)),
                  _whole_spec((CONV_WIDTH, D_MODEL)), _whole_spec((D_MODEL, D_MODEL)),
                  _whole_spec((D_MODEL, 2 * FFN_HIDDEN)), _whole_spec((FFN_HIDDEN, D_MODEL))],
        out_specs=pl.BlockSpec((FFN_TILE, D_MODEL), lambda i, l: (i, 0)),
    )
    return pl.pallas_call(
        functools.partial(_ffn_kernel, final, first_tile),
        grid_spec=grid_spec,
        out_shape=jax.ShapeDtypeStruct((n_tiles * FFN_TILE, D_MODEL), F32),
        compiler_params=pltpu.CompilerParams(
            dimension_semantics=("arbitrary",), vmem_limit_bytes=VMEM_LIMIT),
        name="merge_ffn",
    )(layer, x_p, x_s, y_p, y_s, mods, g1, g2, fg, b_gate, *weights)


def _block_diag(w_pool):
    out = jnp.zeros((DEPTH, POOL_WIDTH, POOL_WIDTH), w_pool.dtype)
    for g in range(POOL_GROUPS):
        a = g * POOL_GROUP_DIM
        out = out.at[:, a:a + POOL_GROUP_DIM, a:a + POOL_GROUP_DIM].set(w_pool[:, g])
    return out


def kernel(x_prompt, x_sample, cache_k, cache_v, c, c_ctx, w_mod, b_mod, norm1_g, norm2_g, w_in,
           b_gate, rpb, w_oa, w_pool, pool_scale, w_ob, w_dw, b_dw, conv_norm_g, conv_norm_b,
           w_oc, w_out, w_gu, w_down, final_g):
    cond =jnp.concatenate([c_ctx[None, :], c,
                            jnp.zeros((MOD_ROWS - 1 - DEC_BATCH, D_MODEL), F32)], axis=0)
    mods = _modulation(cond, w_mod, b_mod)
    bias = _bias_tables(rpb)
    ck = cache_k.reshape(DEC_BATCH, DEPTH, PAST_LEN, NA_WIDTH)
    cv = cache_v.reshape(DEC_BATCH, DEPTH, PAST_LEN, NA_WIDTH)
    g1 = norm1_g.reshape(DEPTH, 1, D_MODEL)
    g2 = norm2_g.reshape(DEPTH, 1, D_MODEL)
    fg = final_g.reshape(1, D_MODEL)
    bg = b_gate.reshape(DEPTH, 1, N_GATE)
    small = (_block_diag(w_pool), pool_scale.reshape(DEPTH, 1, POOL_WIDTH), w_dw,
             b_dw.reshape(DEPTH, 1, CONV_WIDTH), conv_norm_g.reshape(DEPTH, 1, CONV_WIDTH),
             conv_norm_b.reshape(DEPTH, 1, CONV_WIDTH))
    cast_w = (w_in, w_oa, w_ob, w_oc, w_out, w_gu, w_down)
    x_p = x_prompt.reshape(N_PROMPT_TOK, D_MODEL)
    x_s = x_sample.reshape(N_SAMPLE_TOK, D_MODEL)
    sample_row0 = 0
    kbuf = vbuf = None
    for l in range(DEPTH):
        layer = jnp.full((1,), l, jnp.int32)
        y_p, kbuf, vbuf, wmix, *ffn_w = _mix_prompt(layer, x_p, mods, g1, w_in, small, cast_w,
                                                    kbuf, vbuf)
        y_s = _mix_sample(layer, x_s, sample_row0 // DEC_SEQ, mods, g1, wmix, small, ck, cv, bias)
        ffn_args = (layer, x_p, x_s, sample_row0 // FFN_TILE, y_p, y_s, mods, g1, g2, fg, bg, ffn_w)
        if l < DEPTH - 1:
            x_p = x_s = _ffn(*ffn_args, False)
            sample_row0 = N_PROMPT_TOK
    y_prompt = _ffn(*ffn_args, True, 0, PROMPT_TILES).reshape(BATCH, SEQ, D_MODEL)
    y_sample = _ffn(*ffn_args, True, PROMPT_TILES, SAMPLE_TILES).reshape(
        DEC_BATCH, DEC_SEQ, D_MODEL)
    kv_shape = (BATCH, DEPTH, SEQ, NA_HEADS, NA_HEAD_DIM)
    new_k = kbuf.reshape(kv_shape)
    new_v = vbuf.reshape(kv_shape)
    return (y_prompt, y_sample, new_k, new_v)
```

```python
import functools

import jax
import jax.numpy as jnp
from jax import lax
from jax.experimental import pallas as pl
from jax.experimental.pallas import tpu as pltpu

D_MODEL = 1024
BATCH = 16
SEQ = 256
DEPTH = 4
DEC_BATCH = 2
DEC_SEQ = 1024
PAST_LEN = 256
GRID_W = 64
GRID_ROWS = DEC_SEQ // GRID_W
NA_HEADS = 8
NA_HEAD_DIM = 64
NA_WIDTH = NA_HEADS * NA_HEAD_DIM
NA_KH = 8
NA_KW = 16
POOL_WIDTH = 256
POOL_GROUPS = 4
POOL_GROUP_DIM = 64
CONV_WIDTH = 256
CONV_K = 31
N_BRANCH = 3
N_MIX = 3 * NA_WIDTH + POOL_WIDTH + 2 * CONV_WIDTH
N_GATE = N_BRANCH * D_MODEL
N_IN = N_MIX + N_GATE
FFN_HIDDEN = 2816
N_MOD = 6
EPS = 1e-6
NEG_INF = -1e30

N_PROMPT_TOK = BATCH * SEQ
N_SAMPLE_TOK = DEC_BATCH * DEC_SEQ
N_TOK = N_PROMPT_TOK + N_SAMPLE_TOK

LANES = 128
SUBLANES = 8
MOD_ROWS = 8
MOD_BLOCK_N = 1536
TOK_TILE = 256
ATT_BLOCK_ROWS = 4
BIAS_TILES = 16
POOL_PAD = 8
CONV_PAD = 16
MIX_CHUNK = 128
FFN_CHUNKS = ((0, 1024), (1024, 2048), (2048, 2816))
VMEM_LIMIT = 56 * 1024 * 1024

F32 = jnp.float32
BF16 = jnp.bfloat16

CAST_WEIGHTS = ((D_MODEL, N_IN), (NA_WIDTH, D_MODEL), (POOL_WIDTH, D_MODEL), (CONV_WIDTH, D_MODEL),
                (D_MODEL, D_MODEL), (D_MODEL, 2 * FFN_HIDDEN), (FFN_HIDDEN, D_MODEL))


def _sigmoid(x):
    return 1.0 / (1.0 + jnp.exp(-x))


def _rms(x):
    return x * lax.rsqrt(jnp.mean(x * x, axis=-1, keepdims=True) + EPS)


def _dot(a, b):
    return jnp.dot(a, b, preferred_element_type=F32)


def _dot_nt(a, b):
    return lax.dot_general(a, b, (((1,), (1,)), ((), ())), preferred_element_type=F32)


def _layer_spec(shape):
    zeros = (0,) * len(shape)
    return pl.BlockSpec((None,) + tuple(shape), lambda i, l: (l[0],) + zeros,
                        pipeline_mode=pl.Buffered(1))


def _row_spec(width):
    return _whole_spec((DEPTH, width))


def _layer_row(ref, layer):
    return ref.at[pl.ds(layer, 1)]


def _whole_spec(shape):
    zeros = (0,) * len(shape)
    return pl.BlockSpec(tuple(shape), lambda i, l: zeros, pipeline_mode=pl.Buffered(1))


def _mod_kernel(cond_ref, w_ref, b_ref, out_ref):
    c = cond_ref[...]
    s = (c * _sigmoid(c)).astype(BF16)
    out_ref[...] = _dot(s, w_ref[...].astype(BF16)) + b_ref[pl.ds(pl.program_id(0), 1), :]


def _modulation(cond, w_mod, b_mod):
    n = N_MOD * D_MODEL
    return pl.pallas_call(
        _mod_kernel,
        grid=(DEPTH, n // MOD_BLOCK_N),
        in_specs=[
            pl.BlockSpec((MOD_ROWS, D_MODEL), lambda l, j: (0, 0)),
            pl.BlockSpec((None, D_MODEL, MOD_BLOCK_N), lambda l, j: (l, 0, j)),
            pl.BlockSpec((DEPTH, MOD_BLOCK_N), lambda l, j: (0, j)),
        ],
        out_specs=pl.BlockSpec((None, MOD_ROWS, MOD_BLOCK_N), lambda l, j: (l, 0, j)),
        out_shape=jax.ShapeDtypeStruct((DEPTH, MOD_ROWS, n), F32),
        compiler_params=pltpu.CompilerParams(
            dimension_semantics=("arbitrary", "arbitrary"), vmem_limit_bytes=VMEM_LIMIT),
        name="modulation",
    )(cond, w_mod, b_mod)


def _bias_kernel(rpb_ref, out_ref):
    cq = lax.broadcasted_iota(jnp.int32, (GRID_W, LANES), 0)
    ln = lax.broadcasted_iota(jnp.int32, (GRID_W, LANES), 1)
    ck = jnp.where(ln < GRID_W, ln, ln - GRID_W)
    cs = jnp.clip(cq - NA_KW // 2, 0, GRID_W - NA_KW)
    col_in = (ck >= cs) & (ck < cs + NA_KW)
    low_half = ln < GRID_W
    neg = jnp.full((GRID_W, LANES), NEG_INF, F32)
    lo = GRID_W - NA_KW

    for head in range(NA_HEADS):
        rp = rpb_ref[head]
        lane = lax.broadcasted_iota(jnp.int32, rp.shape, 1)
        rolled = pltpu.roll(rp, lo, axis=1)
        first = rp[:, 0:1]
        last = rp[:, 2 * NA_KW - 2:2 * NA_KW - 1]
        ext = jnp.where(lane < lo, first, jnp.where(lane > lo + 2 * NA_KW - 2, last, rolled))

        def toeplitz(tile, shift):
            dr = tile - (NA_KH - 1)
            if dr < -(NA_KH - 1) or dr > NA_KH - 1:
                return neg
            row = jnp.broadcast_to(ext[dr + NA_KH - 1:dr + NA_KH, :], (GRID_W, LANES))
            return pltpu.roll(row, shift, axis=1, stride=1, stride_axis=0)

        for copy in range(2):
            for col in range(BIAS_TILES // 2):
                t_lo = 2 * col + copy
                t_hi = t_lo + 1
                a = toeplitz(t_lo, GRID_W + 1)
                b = toeplitz(t_hi, 1)
                pair = jnp.where(low_half, a, b)
                pair = jnp.where(col_in, pair, NEG_INF)
                out_ref[head, copy, :, col * LANES:(col + 1) * LANES] = pair


def _bias_tables(rpb):
    rp = jnp.pad(rpb, ((0, 0), (0, 0), (0, 1), (0, LANES - (2 * NA_KW - 1))))
    width = BIAS_TILES * GRID_W
    return pl.pallas_call(
        _bias_kernel,
        grid=(DEPTH,),
        in_specs=[pl.BlockSpec((None, NA_HEADS, 2 * NA_KH, LANES), lambda l: (l, 0, 0, 0))],
        out_specs=pl.BlockSpec((None, NA_HEADS, 2, GRID_W, width), lambda l: (l, 0, 0, 0, 0)),
        out_shape=jax.ShapeDtypeStruct((DEPTH, NA_HEADS, 2, GRID_W, width), F32),
        compiler_params=pltpu.CompilerParams(
            dimension_semantics=("arbitrary",), vmem_limit_bytes=VMEM_LIMIT),
        name="bias_tables",
    )(rp)


def _in_proj(x_ref, mod_ref, row, g1_ref, win_ref, u_ref, n_tok):
    shift1 = mod_ref[pl.ds(row, 1), 0:D_MODEL]
    scale1 = mod_ref[pl.ds(row, 1), D_MODEL:2 * D_MODEL]
    for t0 in range(0, n_tok, TOK_TILE):
        x = x_ref[t0:t0 + TOK_TILE, :]
        h = ((_rms(x) * g1_ref[...]) * (1.0 + scale1) + shift1).astype(BF16)
        u_ref[t0:t0 + TOK_TILE, :] = _dot(h, win_ref[...])


def _softmax_pv(parts):
    m = None
    for s, _ in parts:
        mi = jnp.max(s, axis=-1, keepdims=True)
        m = mi if m is None else jnp.maximum(m, mi)
    den = None
    acc = None
    for s, v in parts:
        e = jnp.exp(s - m)
        li = jnp.sum(e, axis=-1, keepdims=True)
        oi = _dot(e.astype(BF16), v)
        den = li if den is None else den + li
        acc = oi if acc is None else acc + oi
    return acc / den


def _context_attention(u_ref, kv_ref, y_ref, n_tok):
    scale = NA_HEAD_DIM ** -0.5
    lane = lax.broadcasted_iota(jnp.int32, (n_tok, LANES), 1)
    low = lane < NA_HEAD_DIM
    for hp in range(NA_HEADS // 2):
        c0 = hp * LANES
        qpair = u_ref[:, c0:c0 + LANES] * scale
        kpair = kv_ref[:, c0:c0 + LANES]
        vpair = kv_ref[:, NA_WIDTH + c0:NA_WIDTH + c0 + LANES]
        outs = []
        for hh in range(2):
            q = jnp.where(low if hh == 0 else ~low, qpair, 0.0).astype(BF16)
            outs.append(_softmax_pv([(_dot_nt(q, kpair), vpair)]))
        y_ref[:, c0:c0 + LANES] = jnp.where(low, outs[0], outs[1]).astype(BF16)


def _window_start(r):
    return min(max(r - NA_KH // 2, 0), GRID_ROWS - NA_KH)


def _neighbourhood_attention(u_ref, kv_ref, ctx_ref, bias_ref, y_ref):
    scale = NA_HEAD_DIM ** -0.5
    blk_tok = ATT_BLOCK_ROWS * GRID_W
    lane = lax.broadcasted_iota(jnp.int32, (blk_tok, LANES), 1)
    low = lane < NA_HEAD_DIM
    for blk in range(GRID_ROWS // ATT_BLOCK_ROWS):
        r0 = blk * ATT_BLOCK_ROWS
        first_row = _window_start(r0)
        key_rows = _window_start(r0 + ATT_BLOCK_ROWS - 1) + NA_KH - first_row
        key_rows += key_rows % 2
        j0 = min(first_row, GRID_ROWS - key_rows)
        key_tok = key_rows * GRID_W
        klane = lax.broadcasted_iota(jnp.int32, (GRID_W, key_tok), 1)
        q0 = r0 * GRID_W
        k0 = j0 * GRID_W
        for hp in range(NA_HEADS // 2):
            c0 = hp * LANES
            qpair = u_ref[q0:q0 + blk_tok, c0:c0 + LANES] * scale
            kpair = kv_ref[k0:k0 + key_tok, c0:c0 + LANES]
            vpair = kv_ref[k0:k0 + key_tok, NA_WIDTH + c0:NA_WIDTH + c0 + LANES]
            kctx = ctx_ref[:, c0:c0 + LANES]
            vctx = ctx_ref[:, NA_WIDTH + c0:NA_WIDTH + c0 + LANES]
            outs = []
            for hh in range(2):
                head = 2 * hp + hh
                q = jnp.where(low if hh == 0 else ~low, qpair, 0.0).astype(BF16)
                s_loc = _dot_nt(q, kpair)
                s_ctx = _dot_nt(q, kctx)
                strips = []
                for rr in range(ATT_BLOCK_ROWS):
                    r = r0 + rr
                    rs = _window_start(r)
                    tile0 = NA_KH - 1 + j0 - r
                    assert 0 <= tile0 and tile0 + key_rows <= BIAS_TILES
                    copy = tile0 % 2
                    b0 = (tile0 - copy) * GRID_W
                    bias = bias_ref[head, copy, :, b0:b0 + key_tok]
                    s = s_loc[rr * GRID_W:(rr + 1) * GRID_W, :] + bias
                    if key_rows > NA_KH:
                        valid = ((klane >= (rs - j0) * GRID_W)
                                 & (klane < (rs - j0 + NA_KH) * GRID_W))
                        s = jnp.where(valid, s, NEG_INF)
                    else:
                        assert rs == j0
                    strips.append(s)
                s_loc = jnp.concatenate(strips, axis=0)
                outs.append(_softmax_pv([(s_loc, vpair), (s_ctx, vctx)]))
            y_ref[q0:q0 + blk_tok, c0:c0 + LANES] = jnp.where(low, outs[0], outs[1]).astype(BF16)


def _pool_mixer(u_ref, pp_ref, wpool_ref, pscale_ref, y_ref, n_tok):
    c_in = 3 * NA_WIDTH
    zeros = jnp.zeros((POOL_PAD, POOL_WIDTH), F32)
    pp_ref[0:POOL_PAD, :] = zeros
    pp_ref[POOL_PAD + n_tok:2 * POOL_PAD + n_tok, :] = zeros
    pp_ref[POOL_PAD:POOL_PAD + n_tok, :] = u_ref[:, c_in:c_in + POOL_WIDTH]
    lane = lax.broadcasted_iota(jnp.int32, (MIX_CHUNK, LANES), 1)
    first = lane < POOL_GROUP_DIM
    tok = lax.broadcasted_iota(jnp.int32, (MIX_CHUNK, LANES), 0)
    wpool = wpool_ref[...].astype(BF16)

    for t0 in range(0, n_tok, MIX_CHUNK):
        def ld(d, half):
            a = POOL_PAD + t0 + d
            return pp_ref[a:a + MIX_CHUNK, half * LANES:(half + 1) * LANES]

        def window(half, lo, hi):
            acc = None
            for d in range(lo, hi):
                acc = ld(d, half) if acc is None else acc + ld(d, half)
            return acc

        a2 = window(0, -1, 1)
        a4 = a2 + ld(-2, 0) + ld(1, 0)
        a8 = window(1, -4, 4)
        a16 = a8 + window(1, -8, -4) + window(1, 4, 8)
        t = tok + t0
        halves = []
        for half, (small, big, ws, wb) in enumerate(((a2, a4, 1, 2), (a8, a16, 4, 8))):
            hw = jnp.where(first, ws, wb)
            cnt = (jnp.minimum(t + hw, n_tok) - jnp.maximum(t - hw, 0)).astype(F32)
            halves.append(jnp.where(first, small, big) / cnt - ld(0, half))
        pooled = jnp.concatenate(halves, axis=1).astype(BF16)
        y = _dot(pooled, wpool) * pscale_ref[...]
        y_ref[t0:t0 + MIX_CHUNK, NA_WIDTH:NA_WIDTH + POOL_WIDTH] = y.astype(BF16)


def _conv_mixer(u_ref, hp_ref, sh_ref, wdw_ref, bdw_ref, cng_ref, cnb_ref, y_ref, n_tok):
    c_a = 3 * NA_WIDTH + POOL_WIDTH
    c_g = c_a + CONV_WIDTH
    zeros = jnp.zeros((CONV_PAD, CONV_WIDTH), F32)
    hp_ref[0:CONV_PAD, :] = zeros
    hp_ref[CONV_PAD + n_tok:2 * CONV_PAD + n_tok, :] = zeros
    for t0 in range(0, n_tok, TOK_TILE):
        a = u_ref[t0:t0 + TOK_TILE, c_a:c_a + CONV_WIDTH]
        g = u_ref[t0:t0 + TOK_TILE, c_g:c_g + CONV_WIDTH]
        hp_ref[CONV_PAD + t0:CONV_PAD + t0 + TOK_TILE, :] = a * _sigmoid(g)
    off = CONV_PAD - CONV_K // 2
    span = MIX_CHUNK + (off + CONV_K - 1) // SUBLANES * SUBLANES
    for t0 in range(0, n_tok, MIX_CHUNK):
        for s in range(1, SUBLANES):
            sh_ref[s, 0:span, :] = hp_ref[t0 + s:t0 + s + span, :]
        acc = None
        for k in range(CONV_K):
            a, s = divmod(off + k, SUBLANES)
            if s == 0:
                win = hp_ref[t0 + a * SUBLANES:t0 + a * SUBLANES + MIX_CHUNK, :]
            else:
                win = sh_ref[s, a * SUBLANES:a * SUBLANES + MIX_CHUNK, :]
            term = wdw_ref[k:k + 1, :] * win
            acc = term if acc is None else acc + term
        acc = acc + bdw_ref[...]
        xc = acc - jnp.mean(acc, axis=-1, keepdims=True)
        y = xc * lax.rsqrt(jnp.mean(xc * xc, axis=-1, keepdims=True) + EPS)
        y = y * cng_ref[...] + cnb_ref[...]
        y_ref[t0:t0 + MIX_CHUNK, NA_WIDTH + POOL_WIDTH:D_MODEL] = (y * _sigmoid(y)).astype(BF16)


def _mix_prompt_kernel(first_layer, layer_ref, x_ref, mod_ref, g1_ref, win_ref, wpool_ref, pscale_ref,
                       wdw_ref, bdw_ref, cng_ref, cnb_ref,
                       cin_ref, coa_ref, cob_ref, coc_ref, cout_ref, cgu_ref, cdown_ref, *rest):
    layer = layer_ref[0]
    g1_ref, pscale_ref, bdw_ref, cng_ref, cnb_ref = (
        _layer_row(r, layer) for r in (g1_ref, pscale_ref, bdw_ref, cng_ref, cnb_ref))
    if not first_layer:
        rest = rest[2:]
    (y_ref, k_ref, v_ref, wmix_ref, bg_ref, boa_ref, bob_ref, boc_ref, bout_ref, bgu_ref, bdown_ref,
     u_ref, kv_ref, pp_ref, hp_ref, sh_ref) = rest
    if first_layer:
        for ref in (k_ref, v_ref):
            ref[1:DEPTH] = jnp.zeros((DEPTH - 1, SEQ, NA_WIDTH), F32)
        k_ref, v_ref = k_ref.at[0], v_ref.at[0]

    @pl.when(pl.program_id(0) == 0)
    def _():
        wmix_ref[...] = win_ref[...].astype(BF16)

    bg_ref[...] = cin_ref[:, N_MIX:].astype(BF16)
    for src, dst in ((coa_ref, boa_ref), (cob_ref, bob_ref), (coc_ref, boc_ref),
                     (cout_ref, bout_ref), (cgu_ref, bgu_ref), (cdown_ref, bdown_ref)):
        dst[...] = src[...].astype(BF16)

    _in_proj(x_ref, mod_ref, 0, g1_ref, wmix_ref, u_ref, SEQ)
    k_ref[...] = u_ref[:, NA_WIDTH:2 * NA_WIDTH]
    v_ref[...] = u_ref[:, 2 * NA_WIDTH:3 * NA_WIDTH]
    kv_ref[...] = u_ref[:, NA_WIDTH:3 * NA_WIDTH].astype(BF16)
    _context_attention(u_ref, kv_ref, y_ref, SEQ)
    _pool_mixer(u_ref, pp_ref, wpool_ref, pscale_ref, y_ref, SEQ)
    _conv_mixer(u_ref, hp_ref, sh_ref, wdw_ref, bdw_ref, cng_ref, cnb_ref, y_ref, SEQ)


def _mix_sample_kernel(layer_ref, x_ref, mod_ref, g1_ref, wmix_ref, wpool_ref, pscale_ref, wdw_ref,
                       bdw_ref, cng_ref, cnb_ref, ck_ref, cv_ref, bias_ref, y_ref,
                       u_ref, kv_ref, ctx_ref, pp_ref, hp_ref, sh_ref):
    layer = layer_ref[0]
    g1_ref, pscale_ref, bdw_ref, cng_ref, cnb_ref = (
        _layer_row(r, layer) for r in (g1_ref, pscale_ref, bdw_ref, cng_ref, cnb_ref))
    row = 1 + pl.program_id(0)
    _in_proj(x_ref, mod_ref, row, g1_ref, wmix_ref, u_ref, DEC_SEQ)
    kv_ref[...] = u_ref[:, NA_WIDTH:3 * NA_WIDTH].astype(BF16)
    ctx_ref[:, 0:NA_WIDTH] = ck_ref[...].astype(BF16)
    ctx_ref[:, NA_WIDTH:2 * NA_WIDTH] = cv_ref[...].astype(BF16)
    _neighbourhood_attention(u_ref, kv_ref, ctx_ref, bias_ref, y_ref)
    _pool_mixer(u_ref, pp_ref, wpool_ref, pscale_ref, y_ref, DEC_SEQ)
    _conv_mixer(u_ref, hp_ref, sh_ref, wdw_ref, bdw_ref, cng_ref, cnb_ref, y_ref, DEC_SEQ)


def _small_weight_specs():
    return [_layer_spec((POOL_WIDTH, POOL_WIDTH)), _row_spec(POOL_WIDTH),
            _layer_spec((CONV_K, CONV_WIDTH)), _row_spec(CONV_WIDTH),
            _row_spec(CONV_WIDTH), _row_spec(CONV_WIDTH)]


def _mix_scratch(n_tok):
    return [
        pltpu.VMEM((n_tok, N_MIX), F32),
        pltpu.VMEM((n_tok, 2 * NA_WIDTH), BF16),
    ], [
        pltpu.VMEM((n_tok + 2 * POOL_PAD, POOL_WIDTH), F32),
        pltpu.VMEM((n_tok + 2 * CONV_PAD, CONV_WIDTH), F32),
        pltpu.VMEM((SUBLANES, MIX_CHUNK + 2 * CONV_PAD, CONV_WIDTH), F32),
    ]


def _mix_prompt(layer, x_src, mods, g1, w_in, small, cast_w, kbuf=None, vbuf=None):
    head, tail = _mix_scratch(SEQ)
    first_layer = kbuf is None
    operands = (layer, x_src, mods, g1, w_in, *small, *cast_w)
    kv_shape = (BATCH, DEPTH, SEQ, NA_WIDTH)
    if first_layer:
        kv_in_specs, aliases = [], {}
        kv_spec = pl.BlockSpec((None, DEPTH, SEQ, NA_WIDTH), lambda b, l: (b, 0, 0, 0))
    else:
        operands += (kbuf, vbuf)
        kv_in_specs = [pl.BlockSpec(memory_space=pl.ANY), pl.BlockSpec(memory_space=pl.ANY)]
        aliases = {len(operands) - 2: 1, len(operands) - 1: 2}
        kv_spec = pl.BlockSpec((None, None, SEQ, NA_WIDTH), lambda b, l: (b, l[0], 0, 0))
    chunk_specs, bf_specs, bf_shapes = [], [], []
    for rows, cols in CAST_WEIGHTS:
        r = rows // BATCH
        chunk_specs.append(pl.BlockSpec((None, r, cols), lambda b, l: (l[0], b, 0)))
        out_cols = N_GATE if cols == N_IN else cols
        bf_specs.append(pl.BlockSpec((r, out_cols), lambda b, l: (b, 0)))
        bf_shapes.append(jax.ShapeDtypeStruct((rows, out_cols), BF16))
    grid_spec = pltpu.PrefetchScalarGridSpec(
        num_scalar_prefetch=1,
        grid=(BATCH,),
        in_specs=[pl.BlockSpec((SEQ, D_MODEL), lambda b, l: (b, 0)),
                  _layer_spec((MOD_ROWS, N_MOD * D_MODEL)),
                  _row_spec(D_MODEL),
                  _layer_spec((D_MODEL, N_MIX))] + _small_weight_specs() + chunk_specs
        + kv_in_specs,
        out_specs=[pl.BlockSpec((SEQ, D_MODEL), lambda b, l: (b, 0)), kv_spec, kv_spec,
                   pl.BlockSpec((D_MODEL, N_MIX), lambda b, l: (0, 0))] + bf_specs,
        scratch_shapes=head + tail,
    )
    return pl.pallas_call(
        functools.partial(_mix_prompt_kernel, first_layer),
        grid_spec=grid_spec,
        out_shape=[jax.ShapeDtypeStruct((N_PROMPT_TOK, D_MODEL), BF16),
                   jax.ShapeDtypeStruct(kv_shape, F32),
                   jax.ShapeDtypeStruct(kv_shape, F32),
                   jax.ShapeDtypeStruct((D_MODEL, N_MIX), BF16)] + bf_shapes,
        input_output_aliases=aliases,
        compiler_params=pltpu.CompilerParams(
            dimension_semantics=("arbitrary",), vmem_limit_bytes=VMEM_LIMIT),
        name="mix_prompt",
    )(*operands)


def _mix_sample(layer, x_src, first_blk, mods, g1, wmix, small, cache_k, cache_v, bias):
    head, tail = _mix_scratch(DEC_SEQ)
    ctx_spec = pl.BlockSpec((None, None, PAST_LEN, NA_WIDTH), lambda b, l: (b, l[0], 0, 0))
    grid_spec = pltpu.PrefetchScalarGridSpec(
        num_scalar_prefetch=1,
        grid=(DEC_BATCH,),
        in_specs=[pl.BlockSpec((DEC_SEQ, D_MODEL), lambda b, l: (first_blk + b, 0)),
                  _layer_spec((MOD_ROWS, N_MOD * D_MODEL)),
                  _row_spec(D_MODEL),
                  _whole_spec((D_MODEL, N_MIX))] + _small_weight_specs()
        + [ctx_spec, ctx_spec, _layer_spec((NA_HEADS, 2, GRID_W, BIAS_TILES * GRID_W))],
        out_specs=pl.BlockSpec((DEC_SEQ, D_MODEL), lambda b, l: (b, 0)),
        scratch_shapes=head + [pltpu.VMEM((PAST_LEN, 2 * NA_WIDTH), BF16)] + tail,
    )
    return pl.pallas_call(
        _mix_sample_kernel,
        grid_spec=grid_spec,
        out_shape=jax.ShapeDtypeStruct((N_SAMPLE_TOK, D_MODEL), BF16),
        compiler_params=pltpu.CompilerParams(
            dimension_semantics=("arbitrary",), vmem_limit_bytes=VMEM_LIMIT),
        name="mix_sample",
    )(layer, x_src, mods, g1, wmix, *small, cache_k, cache_v, bias)


FFN_TILE = 2 * TOK_TILE
PROMPT_TILES = N_PROMPT_TOK // FFN_TILE
SAMPLE_TILES = N_SAMPLE_TOK // FFN_TILE


def _ffn_kernel(final, first_tile, layer_ref, xp_ref, xs_ref, yp_ref, ys_ref, mod_ref, g1_ref, g2_ref,
                fg_ref, bg_ref, wg_ref, woa_ref, wob_ref, woc_ref, wout_ref, wgu_ref, wdown_ref,
                o_ref):
    layer = layer_ref[0]
    g1_ref, g2_ref, bg_ref = (_layer_row(r, layer) for r in (g1_ref, g2_ref, bg_ref))
    i = pl.program_id(0) + first_tile
    is_prompt = i < PROMPT_TILES
    row = jnp.where(is_prompt, 0, 1 + (i - PROMPT_TILES) // (DEC_SEQ // FFN_TILE))

    def mod(j):
        return mod_ref[pl.ds(row, 1), j * D_MODEL:(j + 1) * D_MODEL]

    branches = ((0, NA_WIDTH, woa_ref), (NA_WIDTH, NA_WIDTH + POOL_WIDTH, wob_ref),
                (NA_WIDTH + POOL_WIDTH, D_MODEL, woc_ref))
    halves = [slice(t0, t0 + TOK_TILE) for t0 in range(0, FFN_TILE, TOK_TILE)]
    xs = [jnp.where(is_prompt, xp_ref[rows, :], xs_ref[rows, :]) for rows in halves]
    hs = [((_rms(x) * g1_ref[...]) * (1.0 + mod(1)) + mod(0)).astype(BF16) for x in xs]
    merged = [None] * len(halves)
    for br, (c0, c1, w_ref) in enumerate(branches):
        g0 = br * D_MODEL
        gates = [_sigmoid(_dot(h, wg_ref[:, g0:g0 + D_MODEL]) + bg_ref[:, g0:g0 + D_MODEL])
                 for h in hs]
        ys = [jnp.where(is_prompt, yp_ref[rows, c0:c1], ys_ref[rows, c0:c1]) for rows in halves]
        terms = [gate * _dot(y, w_ref[...]) for gate, y in zip(gates, ys)]
        merged = [t if m is None else m + t for m, t in zip(merged, terms)]
    x1s = [x + mod(2) * _dot(m.astype(BF16), wout_ref[...]) for x, m in zip(xs, merged)]
    h2s = [((_rms(x1) * g2_ref[...]) * (1.0 + mod(4)) + mod(3)).astype(BF16) for x1 in x1s]
    accs = [None] * len(halves)
    for c0, c1 in FFN_CHUNKS:
        a_s = [_dot(h2, wgu_ref[:, c0:c1]) for h2 in h2s]
        b_s = [_dot(h2, wgu_ref[:, FFN_HIDDEN + c0:FFN_HIDDEN + c1]) for h2 in h2s]
        ts = [((a * _sigmoid(a)) * b).astype(BF16) for a, b in zip(a_s, b_s)]
        ds = [_dot(t, wdown_ref[c0:c1, :]) for t in ts]
        accs = [d if acc is None else acc + d for acc, d in zip(accs, ds)]
    for rows, x1, acc in zip(halves, x1s, accs):
        x2 = x1 + mod(5) * acc
        if final:
            x2 = _rms(x2) * fg_ref[...]
        o_ref[rows, :] = x2


def _ffn(layer, x_p, x_s, sample_tile0, y_p, y_s, mods, g1, g2, fg, b_gate, weights, final,
         first_tile=0, n_tiles=N_TOK // FFN_TILE):
    f = first_tile
    p_spec = pl.BlockSpec((FFN_TILE, D_MODEL),
                          lambda i, l: (jnp.minimum(i + f, PROMPT_TILES - 1), 0))
    s_spec = pl.BlockSpec((FFN_TILE, D_MODEL),
                          lambda i, l: (jnp.maximum(i + f - PROMPT_TILES, 0), 0))
    xs_spec = pl.BlockSpec((FFN_TILE, D_MODEL),
                           lambda i, l: (sample_tile0 + jnp.maximum(i + f - PROMPT_TILES, 0), 0))
    grid_spec = pltpu.PrefetchScalarGridSpec(
        num_scalar_prefetch=1,
        grid=(n_tiles,),
        in_specs=[p_spec, xs_spec, p_spec, s_spec,
                  _layer_spec((MOD_ROWS, N_MOD * D_MODEL)),
                  _row_spec(D_MODEL), _row_spec(D_MODEL), _whole_spec((1, D_MODEL)),
                  _row_spec(N_GATE),
                  _whole_spec((D_MODEL, N_GATE)),
                  _whole_spec((NA_WIDTH, D_MODEL)), _whole_spec((POOL_WIDTH, D_MODEL)),
                  _whole_spec((CONV_WIDTH, D_MODEL)), _whole_spec((D_MODEL, D_MODEL)),
                  _whole_spec((D_MODEL, 2 * FFN_HIDDEN)), _whole_spec((FFN_HIDDEN, D_MODEL))],
        out_specs=pl.BlockSpec((FFN_TILE, D_MODEL), lambda i, l: (i, 0)),
    )
    return pl.pallas_call(
        functools.partial(_ffn_kernel, final, first_tile),
        grid_spec=grid_spec,
        out_shape=jax.ShapeDtypeStruct((n_tiles * FFN_TILE, D_MODEL), F32),
        compiler_params=pltpu.CompilerParams(
            dimension_semantics=("arbitrary",), vmem_limit_bytes=VMEM_LIMIT),
        name="merge_ffn",
    )(layer, x_p, x_s, y_p, y_s, mods, g1, g2, fg, b_gate, *weights)


def _block_diag(w_pool):
    out = jnp.zeros((DEPTH, POOL_WIDTH, POOL_WIDTH), w_pool.dtype)
    for g in range(POOL_GROUPS):
        a = g * POOL_GROUP_DIM
        out = out.at[:, a:a + POOL_GROUP_DIM, a:a + POOL_GROUP_DIM].set(w_pool[:, g])
    return out


def kernel(x_prompt, x_sample, cache_k, cache_v, c, c_ctx, w_mod, b_mod, norm1_g, norm2_g, w_in,
           b_gate, rpb, w_oa, w_pool, pool_scale, w_ob, w_dw, b_dw, conv_norm_g, conv_norm_b,
           w_oc, w_out, w_gu, w_down, final_g):
    cond =jnp.concatenate([c_ctx[None, :], c,
                            jnp.zeros((MOD_ROWS - 1 - DEC_BATCH, D_MODEL), F32)], axis=0)
    mods = _modulation(cond, w_mod, b_mod)
    bias = _bias_tables(rpb)
    ck = cache_k.reshape(DEC_BATCH, DEPTH, PAST_LEN, NA_WIDTH)
    cv = cache_v.reshape(DEC_BATCH, DEPTH, PAST_LEN, NA_WIDTH)
    g1, g2, bg = norm1_g, norm2_g, b_gate
    fg = final_g.reshape(1, D_MODEL)
    small = (_block_diag(w_pool), pool_scale, w_dw, b_dw, conv_norm_g, conv_norm_b)
    cast_w = (w_in, w_oa, w_ob, w_oc, w_out, w_gu, w_down)
    x_p = x_prompt.reshape(N_PROMPT_TOK, D_MODEL)
    x_s = x_sample.reshape(N_SAMPLE_TOK, D_MODEL)
    sample_row0 = 0
    kbuf = vbuf = None
    for l in range(DEPTH):
        layer = jnp.full((1,), l, jnp.int32)
        y_p, kbuf, vbuf, wmix, *ffn_w = _mix_prompt(layer, x_p, mods, g1, w_in, small, cast_w,
                                                    kbuf, vbuf)
        y_s = _mix_sample(layer, x_s, sample_row0 // DEC_SEQ, mods, g1, wmix, small, ck, cv, bias)
        ffn_args = (layer, x_p, x_s, sample_row0 // FFN_TILE, y_p, y_s, mods, g1, g2, fg, bg, ffn_w)
        if l < DEPTH - 1:
            x_p = x_s = _ffn(*ffn_args, False)
            sample_row0 = N_PROMPT_TOK
    y_prompt = _ffn(*ffn_args, True, 0, PROMPT_TILES).reshape(BATCH, SEQ, D_MODEL)
    y_sample = _ffn(*ffn_args, True, PROMPT_TILES, SAMPLE_TILES).reshape(
        DEC_BATCH, DEC_SEQ, D_MODEL)
    kv_shape = (BATCH, DEPTH, SEQ, NA_HEADS, NA_HEAD_DIM)
    new_k = kbuf.reshape(kv_shape)
    new_v = vbuf.reshape(kv_shape)
    return (y_prompt, y_sample, new_k, new_v)
```
